```python
import jax, jax.numpy as jnp
from jax import lax
import numpy as np

D_MODEL = 1024
BATCH = 4
SEQ = 8192
DEPTH = 1
DEC_BATCH = 32
DEC_SEQ = 16
PAST_LEN = 1024

CHUNK = 64
EPS = 1e-6
ROPE_THETA = 500000.0
GMLP_CHUNK = 128
GMLP_GROUPS = 8
GMLP_GROUP_DIM = 128
GMLP_WIDTH = GMLP_GROUPS * GMLP_GROUP_DIM
N_HEADS = 8
N_KV_HEADS = 2
Q_PER_KV = N_HEADS // N_KV_HEADS
HEAD_DIM = 128
ROT_DIM = HEAD_DIM // 4
N_IDX_HEADS = 8
IDX_DIM = 64
IDX_ROT_DIM = IDX_DIM // 4
TOPK_MAX = 256
Q_BLOCK = 128
N_BRANCHES = 2
IN_SIZES = (GMLP_WIDTH, GMLP_WIDTH, N_HEADS * HEAD_DIM, N_KV_HEADS * HEAD_DIM, N_KV_HEADS * HEAD_DIM,
            N_IDX_HEADS * IDX_DIM, IDX_DIM, N_IDX_HEADS, N_BRANCHES * D_MODEL)
IN_COLS = sum(IN_SIZES)
PEER_HEADS = 8
PEER_N_KEYS = 128
PEER_N_EXPERTS = PEER_N_KEYS * PEER_N_KEYS
PEER_KEY_DIM = 256
PEER_TOPK = 16
PEER_BLOCK = 256

kernel_name = "chunk_causal_gmlp_dsa_peer_step"


def rmsnorm(x, g):
    x32 = x.astype(jnp.float32)
    y = x32 * lax.rsqrt(jnp.mean(x32 * x32, axis=-1, keepdims=True) + EPS)
    return y.astype(x.dtype) * g


def rope_partial(x, pos, rot_dim):
    half = rot_dim // 2
    inv_freq = ROPE_THETA ** (-jnp.arange(half, dtype=jnp.float32) / half)
    ang = pos.astype(jnp.float32)[..., None] * inv_freq
    cos = jnp.cos(ang).astype(x.dtype)
    sin = jnp.sin(ang).astype(x.dtype)
    x1, x2, xp = x[..., :half], x[..., half:rot_dim], x[..., rot_dim:]
    return jnp.concatenate([x1 * cos - x2 * sin, x2 * cos + x1 * sin, xp], axis=-1)


def adaln(c, w, b):
    return jnp.split(jax.nn.silu(c) @ w + b, 6, axis=-1)


def modulate(x, g, shift, scale):
    return rmsnorm(x, g) * (1 + scale[:, None, :]) + shift[:, None, :]


def split_in(p):
    points = [int(s) for s in np.cumsum(IN_SIZES)[:-1]]
    return jnp.split(p, points, axis=-1)


def mixer_projections(h, pos, w_in, q_norm_g, k_norm_g, kidx_norm_g, gmlp_v_norm_g):
    B, S, _ = h.shape
    u, v, q, k, va, qi, ki, wi, gates = split_in(h @ w_in)
    u = jax.nn.gelu(u)
    v = rmsnorm(jax.nn.gelu(v), gmlp_v_norm_g)
    q = rope_partial(rmsnorm(q.reshape(B, S, N_HEADS, HEAD_DIM), q_norm_g), pos[:, None], ROT_DIM)
    q = q.reshape(B, S, N_KV_HEADS, Q_PER_KV, HEAD_DIM)
    k = rope_partial(rmsnorm(k.reshape(B, S, N_KV_HEADS, HEAD_DIM), k_norm_g), pos[:, None], ROT_DIM)
    va = va.reshape(B, S, N_KV_HEADS, HEAD_DIM)
    qi = rope_partial(qi.reshape(B, S, N_IDX_HEADS, IDX_DIM), pos[:, None], IDX_ROT_DIM)
    ki = rope_partial(rmsnorm(ki, kidx_norm_g), pos, IDX_ROT_DIM)
    wi = wi * N_IDX_HEADS ** -0.5
    return u, v, q, k, va, qi, ki, wi, gates


def gmlp_weights(ws):
    i = jnp.arange(GMLP_CHUNK)
    mask = (i[None, :] // CHUNK) <= (i[:, None] // CHUNK)
    return jnp.where(mask[None], ws, 0)


def gmlp_prompt(u, v, ws, b):
    B, S, _ = u.shape
    vb = v.reshape(B, S // GMLP_CHUNK, GMLP_CHUNK, GMLP_GROUPS, GMLP_GROUP_DIM)
    s = jnp.einsum('gij,bnjgc->bnigc', gmlp_weights(ws), vb) + b.T[:, :, None]
    return u * s.reshape(B, S, GMLP_WIDTH)


def gmlp_sample(u, v, ws, b):
    B, S, _ = u.shape
    vb = v.reshape(B, S, GMLP_GROUPS, GMLP_GROUP_DIM)
    s = jnp.einsum('gij,bjgc->bigc', gmlp_weights(ws)[:, :S, :S], vb) + b.T[:S, :, None]
    return u * s.reshape(B, S, GMLP_WIDTH)


def dsa_attend(q, qi, wi, q_pos, k, v, ki, k_pos, k_sel):
    idx = jax.nn.relu(jnp.einsum('bqhd,bld->bqhl', qi, ki) * IDX_DIM ** -0.5)
    idx = jnp.einsum('bqh,bqhl->bql', wi, idx).astype(jnp.float32)
    admissible = (k_pos[None, :] // CHUNK) <= (q_pos[:, None] // CHUNK)
    idx = jnp.where(admissible[None], idx, -jnp.inf)
    _, sel = lax.top_k(idx, k_sel)
    valid = (k_pos[sel] // CHUNK) <= (q_pos[None, :, None] // CHUNK)
    gather = jax.vmap(lambda rows, ids: rows[ids])
    kg = gather(k, sel)
    vg = gather(v, sel)
    s = jnp.einsum('bqhgd,bqkhd->bqhgk', q, kg).astype(jnp.float32) * HEAD_DIM ** -0.5
    s = jnp.where(valid[:, :, None, None, :], s, -jnp.inf)
    p = jax.nn.softmax(s, axis=-1).astype(v.dtype)
    return jnp.einsum('bqhgk,bqkhd->bqhgd', p, vg)


def dsa_prompt(q, qi, wi, k, va, ki, pos, k_sel):
    B, S = q.shape[:2]
    nblk = S // Q_BLOCK

    def to_blocks(a):
        return jnp.moveaxis(a.reshape((B, nblk, Q_BLOCK) + a.shape[2:]), 1, 0)

    def attend_block(args):
        qb, qib, wib, pb = args
        return dsa_attend(qb, qib, wib, pb, k, va, ki, pos, k_sel)

    o = lax.map(attend_block, (to_blocks(q), to_blocks(qi), to_blocks(wi), pos.reshape(nblk, Q_BLOCK)))
    return jnp.moveaxis(o, 0, 1).reshape(B, S, N_HEADS * HEAD_DIM)


def merge_branches(a_out, b_out, gates, w_branch_a, w_branch_b, w_out):
    g_a, g_b = jnp.split(gates, N_BRANCHES, axis=-1)
    merged = jax.nn.sigmoid(g_a) * (a_out @ w_branch_a) + jax.nn.sigmoid(g_b) * (b_out @ w_branch_b)
    return merged @ w_out


def peer_ffn(h, wq, subkeys, u_tab, v_tab):
    shape = h.shape
    x = h.reshape(-1, D_MODEL)
    T = x.shape[0]
    nb = -(-T // PEER_BLOCK)
    xb_all = jnp.pad(x, ((0, nb * PEER_BLOCK - T), (0, 0))).reshape(nb, PEER_BLOCK, D_MODEL)

    def block(xb):
        q = (xb @ wq).reshape(PEER_BLOCK, PEER_HEADS, 2, PEER_KEY_DIM // 2)
        s = jnp.einsum('thpd,hpnd->thpn', q, subkeys).astype(jnp.float32)
        sv, si = lax.top_k(s, PEER_TOPK)
        cand = (sv[:, :, 0, :, None] + sv[:, :, 1, None, :]).reshape(PEER_BLOCK, PEER_HEADS, PEER_TOPK * PEER_TOPK)
        gv, ci = lax.top_k(cand, PEER_TOPK)
        i1 = jnp.take_along_axis(si[:, :, 0, :], ci // PEER_TOPK, axis=-1)
        i2 = jnp.take_along_axis(si[:, :, 1, :], ci % PEER_TOPK, axis=-1)
        expert = i1 * PEER_N_KEYS + i2
        g = jax.nn.softmax(gv, axis=-1).astype(xb.dtype)
        act = jax.nn.gelu(jnp.einsum('td,thkd->thk', xb, u_tab[expert]))
        return jnp.einsum('thk,thkd->td', g * act, v_tab[expert])

    y = lax.map(block, xb_all).reshape(-1, D_MODEL)[:T]
    return y.reshape(shape)


def setup_inputs(seed: int = 0) -> dict:
    key = jax.random.key(seed)
    ks = jax.random.split(key, 32)

    def nrm(k, shape, scale):
        return scale * jax.random.normal(k, shape, dtype=jnp.float32)

    L = DEPTH
    return {
        "x_prompt": nrm(ks[0], (BATCH, SEQ, D_MODEL), 1.0),
        "x_sample": nrm(ks[1], (DEC_BATCH, DEC_SEQ, D_MODEL), 1.0),
        "cache_k": nrm(ks[2], (L, DEC_BATCH, PAST_LEN, N_KV_HEADS, HEAD_DIM), 1.0),
        "cache_v": nrm(ks[3], (L, DEC_BATCH, PAST_LEN, N_KV_HEADS, HEAD_DIM), 1.0),
        "cache_kidx": nrm(ks[4], (L, DEC_BATCH, PAST_LEN, IDX_DIM), 1.0),
        "c_prompt": nrm(ks[5], (BATCH, D_MODEL), 1.0),
        "c_sample": nrm(ks[6], (DEC_BATCH, D_MODEL), 1.0),
        "ada_w": nrm(ks[7], (L, D_MODEL, 6 * D_MODEL), 0.5 * D_MODEL ** -0.5),
        "ada_b": nrm(ks[8], (L, 6 * D_MODEL), 0.01),
        "norm1_g": 1.0 + nrm(ks[9], (L, D_MODEL), 0.02),
        "norm2_g": 1.0 + nrm(ks[10], (L, D_MODEL), 0.02),
        "w_in": nrm(ks[11], (L, D_MODEL, IN_COLS), D_MODEL ** -0.5),
        "q_norm_g": 1.0 + nrm(ks[12], (L, HEAD_DIM), 0.02),
        "k_norm_g": 1.0 + nrm(ks[13], (L, HEAD_DIM), 0.02),
        "kidx_norm_g": 1.0 + nrm(ks[14], (L, IDX_DIM), 0.02),
        "gmlp_v_norm_g": 1.0 + nrm(ks[15], (L, GMLP_WIDTH), 0.02),
        "gmlp_ws": nrm(ks[16], (L, GMLP_GROUPS, GMLP_CHUNK, GMLP_CHUNK), GMLP_CHUNK ** -0.5),
        "gmlp_b": 1.0 + nrm(ks[17], (L, GMLP_GROUPS, GMLP_CHUNK), 0.02),
        "w_branch_a": nrm(ks[18], (L, GMLP_WIDTH, D_MODEL), GMLP_WIDTH ** -0.5),
        "w_branch_b": nrm(ks[19], (L, N_HEADS * HEAD_DIM, D_MODEL), (N_HEADS * HEAD_DIM) ** -0.5),
        "w_out": nrm(ks[20], (L, D_MODEL, D_MODEL), D_MODEL ** -0.5),
        "peer_wq": nrm(ks[21], (L, D_MODEL, PEER_HEADS * PEER_KEY_DIM), D_MODEL ** -0.5),
        "peer_subkeys": nrm(ks[22], (L, PEER_HEADS, 2, PEER_N_KEYS, PEER_KEY_DIM // 2), (PEER_KEY_DIM // 2) ** -0.5),
        "peer_u": nrm(ks[23], (L, PEER_N_EXPERTS, D_MODEL), D_MODEL ** -0.5),
        "peer_v": nrm(ks[24], (L, PEER_N_EXPERTS, D_MODEL), 0.5),
    }


def reference(x_prompt, x_sample, cache_k, cache_v, cache_kidx, c_prompt, c_sample,
              ada_w, ada_b, norm1_g, norm2_g, w_in, q_norm_g, k_norm_g, kidx_norm_g, gmlp_v_norm_g,
              gmlp_ws, gmlp_b, w_branch_a, w_branch_b, w_out, peer_wq, peer_subkeys, peer_u, peer_v):
    B, S, _ = x_prompt.shape
    DB, DS, _ = x_sample.shape
    P = cache_k.shape[2]
    pos_p = jnp.arange(S)
    pos_s = P + jnp.arange(DS)
    kpos_s = jnp.arange(P + DS)
    ksel_p = min(TOPK_MAX, S // 4)
    ksel_s = min(TOPK_MAX, (P + DS) // 4)
    xp, xs = x_prompt, x_sample
    nk_p, nv_p, ni_p, nk_s, nv_s, ni_s, ng_s = [], [], [], [], [], [], []
    for l in range(DEPTH):
        sh1p, sc1p, gt1p, sh2p, sc2p, gt2p = adaln(c_prompt, ada_w[l], ada_b[l])
        sh1s, sc1s, gt1s, sh2s, sc2s, gt2s = adaln(c_sample, ada_w[l], ada_b[l])

        def proj(h, pos):
            return mixer_projections(h, pos, w_in[l], q_norm_g[l], k_norm_g[l], kidx_norm_g[l], gmlp_v_norm_g[l])

        u, v, q, k, va, qi, ki, wi, gates = proj(modulate(xp, norm1_g[l], sh1p, sc1p), pos_p)
        a_out = gmlp_prompt(u, v, gmlp_ws[l], gmlp_b[l])
        b_out = dsa_prompt(q, qi, wi, k, va, ki, pos_p, ksel_p)
        xp = xp + gt1p[:, None, :] * merge_branches(a_out, b_out, gates, w_branch_a[l], w_branch_b[l], w_out[l])
        nk_p.append(k)
        nv_p.append(va)
        ni_p.append(ki)

        u, v, q, k, va, qi, ki, wi, gates = proj(modulate(xs, norm1_g[l], sh1s, sc1s), pos_s)
        a_out = gmlp_sample(u, v, gmlp_ws[l], gmlp_b[l])
        k_all = jnp.concatenate([cache_k[l], k], axis=1)
        v_all = jnp.concatenate([cache_v[l], va], axis=1)
        ki_all = jnp.concatenate([cache_kidx[l], ki], axis=1)
        b_out = dsa_attend(q, qi, wi, pos_s, k_all, v_all, ki_all, kpos_s, ksel_s).reshape(DB, DS, N_HEADS * HEAD_DIM)
        xs = xs + gt1s[:, None, :] * merge_branches(a_out, b_out, gates, w_branch_a[l], w_branch_b[l], w_out[l])
        nk_s.append(k)
        nv_s.append(va)
        ni_s.append(ki)
        ng_s.append(v)

        xp = xp + gt2p[:, None, :] * peer_ffn(modulate(xp, norm2_g[l], sh2p, sc2p), peer_wq[l], peer_subkeys[l], peer_u[l], peer_v[l])
        xs = xs + gt2s[:, None, :] * peer_ffn(modulate(xs, norm2_g[l], sh2s, sc2s), peer_wq[l], peer_subkeys[l], peer_u[l], peer_v[l])

    return (xp, xs, jnp.stack(nk_p), jnp.stack(nv_p), jnp.stack(ni_p),
            jnp.stack(nk_s), jnp.stack(nv_s), jnp.stack(ni_s), jnp.stack(ng_s))
```

```python
import functools

import jax
import jax.numpy as jnp
import numpy as np
from jax import lax
from jax.experimental import pallas as pl
from jax.experimental.pallas import tpu as pltpu

CHUNK = 64
EPS = 1e-6
ROPE_THETA = 500000.0
GMLP_CHUNK = 128
GMLP_GROUPS = 8
GMLP_GROUP_DIM = 128
GMLP_WIDTH = GMLP_GROUPS * GMLP_GROUP_DIM
N_HEADS = 8
N_KV_HEADS = 2
Q_PER_KV = N_HEADS // N_KV_HEADS
HEAD_DIM = 128
ROT_DIM = HEAD_DIM // 4
N_IDX_HEADS = 8
IDX_DIM = 64
IDX_ROT_DIM = IDX_DIM // 4
TOPK_MAX = 256
PEER_HEADS = 8
PEER_N_KEYS = 128
PEER_KEY_DIM = 256
PEER_TOPK = 16

LANES = 128
SUBLANES = 8
VMEM_LIMIT_BYTES = 56 * 1024 * 1024

QBLK = LANES
KTILE = 128
PEER_TOK = 32
INT_MIN = -(2 ** 31)
NEG_BIG = -1e30
IDX_BIG = 2 ** 30


def _cparams(sem):
    return pltpu.CompilerParams(dimension_semantics=sem, vmem_limit_bytes=VMEM_LIMIT_BYTES)


def _rms(x, width):
    return x * lax.rsqrt(jnp.sum(x * x, axis=-1, keepdims=True) * (1.0 / width) + EPS)


def _modulate(x, g, sc, sh):
    return _rms(x, x.shape[-1]) * g * (1.0 + sc) + sh


def _rope(x, c, sa, sb, half):
    n = x.shape[-1]
    return x * c + pltpu.roll(x, n - half, 1) * sa + pltpu.roll(x, half, 1) * sb


def _bdot(a, b):
    return jnp.dot(a, b, preferred_element_type=jnp.float32)


def _dot_nt(a, b):
    return lax.dot_general(a, b, (((1,), (1,)), ((), ())), preferred_element_type=jnp.float32)


def _adaln_kernel(c_ref, w_ref, b_ref, o_ref):
    c = c_ref[...]
    h = (c * jax.nn.sigmoid(c)).astype(jnp.bfloat16)
    o_ref[...] = _bdot(h, w_ref[...].astype(jnp.bfloat16)) + b_ref[...]


def _adaln(c, w, b):
    rows, d = c.shape
    n = w.shape[1]
    tn = n // 4
    return pl.pallas_call(
        _adaln_kernel,
        grid=(n // tn,),
        in_specs=[pl.BlockSpec((rows, d), lambda j: (0, 0)),
                  pl.BlockSpec((d, tn), lambda j: (0, j)),
                  pl.BlockSpec((1, tn), lambda j: (0, j))],
        out_specs=pl.BlockSpec((rows, tn), lambda j: (0, j)),
        out_shape=jax.ShapeDtypeStruct((rows, n), jnp.float32),
        compiler_params=_cparams(("arbitrary",)),
        name="adaln",
    )(c, w, b.reshape(1, n))


def _inproj_kernel(x_ref, sc_ref, sh_ref, g1_ref,
                   wu_ref, wv_ref, wq_ref, wk_ref, wva_ref, wqi_ref, wg_ref, wkw_ref,
                   c128_ref, sa128_ref, sb128_ref, c64_ref, sa64_ref, sb64_ref,
                   qg_ref, kg_ref, kig_ref, gvg_ref, wm_ref, gb_ref,
                   *out_refs, emit_v):
    if emit_v:
        (a_ref, v_ref, q_ref, k32_ref, k16_ref, va32_ref, va16_ref, qi_ref, kw32_ref, ki16_ref, sg_ref) = out_refs
    else:
        (a_ref, q_ref, k32_ref, k16_ref, va32_ref, va16_ref, qi_ref, kw32_ref, ki16_ref, sg_ref) = out_refs
        v_ref = None
    tm = x_ref.shape[0]
    h = _modulate(x_ref[...], g1_ref[...], sc_ref[...], sh_ref[...]).astype(jnp.bfloat16)

    c128, sa128, sb128 = c128_ref[...], sa128_ref[...], sb128_ref[...]
    c64, sa64, sb64 = c64_ref[...], sa64_ref[...], sb64_ref[...]

    v = _rms(jax.nn.gelu(_bdot(h, wv_ref[...])), GMLP_WIDTH) * gvg_ref[...]
    if v_ref is not None:
        v_ref[...] = v
    v16 = v.astype(jnp.bfloat16)
    u = jax.nn.gelu(_bdot(h, wu_ref[...]))
    for c in range(tm // GMLP_CHUNK):
        rows = slice(c * GMLP_CHUNK, (c + 1) * GMLP_CHUNK)
        for g in range(GMLP_GROUPS):
            cols = slice(g * GMLP_GROUP_DIM, (g + 1) * GMLP_GROUP_DIM)
            s = _bdot(wm_ref[g], v16[rows, cols]) + gb_ref[:, cols]
            a_ref[rows, cols] = (u[rows, cols] * s).astype(jnp.bfloat16)

    q = _bdot(h, wq_ref[...])
    for hd in range(N_HEADS):
        cols = slice(hd * HEAD_DIM, (hd + 1) * HEAD_DIM)
        qh = _rope(_rms(q[:, cols], HEAD_DIM) * qg_ref[...], c128, sa128, sb128, ROT_DIM // 2)
        q_ref[:, cols] = qh.astype(jnp.bfloat16)
    k = _bdot(h, wk_ref[...])
    for hd in range(N_KV_HEADS):
        cols = slice(hd * HEAD_DIM, (hd + 1) * HEAD_DIM)
        kh = _rope(_rms(k[:, cols], HEAD_DIM) * kg_ref[...], c128, sa128, sb128, ROT_DIM // 2)
        k32_ref[:, cols] = kh
        k16_ref[:, cols] = kh.astype(jnp.bfloat16)
    va = _bdot(h, wva_ref[...])
    va32_ref[...] = va
    va16_ref[...] = va.astype(jnp.bfloat16)

    qi = _bdot(h, wqi_ref[...])
    for p in range(N_IDX_HEADS * IDX_DIM // LANES):
        cols = slice(p * LANES, (p + 1) * LANES)
        qi_ref[:, cols] = _rope(qi[:, cols], c64, sa64, sb64, IDX_ROT_DIM // 2).astype(jnp.bfloat16)
    kw = _bdot(h, wkw_ref[...])
    is_ki = lax.broadcasted_iota(jnp.int32, kw.shape, 1) < IDX_DIM
    kis = jnp.where(is_ki, kw, 0.0)
    kin = kis * lax.rsqrt(jnp.sum(kis * kis, axis=-1, keepdims=True) * (1.0 / IDX_DIM) + EPS) * kig_ref[...]
    kir = _rope(kin, c64, sa64, sb64, IDX_ROT_DIM // 2)
    kw32_ref[...] = jnp.where(is_ki, kir, kw * (N_IDX_HEADS ** -0.5))
    ki16_ref[...] = jnp.where(is_ki, kir, 0.0).astype(jnp.bfloat16)

    sg_ref[...] = jax.nn.sigmoid(_bdot(h, wg_ref[...])).astype(jnp.bfloat16)


def _inproj(x2d, sc, sh, g1, wts, tabs, gains, wm, gb, *, tm, tab_blocks, emit_v):
    T, D = x2d.shape
    nt = T // tm
    n_mod, R, _ = sc.shape
    tiles_per_mod = nt // n_mod
    const2 = lambda i: (0, 0)
    mod_spec = pl.BlockSpec((None, R, D), lambda i: (i // tiles_per_mod, 0, 0))
    tab_spec = pl.BlockSpec((tm, LANES), lambda i: (i % tab_blocks, 0))
    in_specs = ([pl.BlockSpec((tm, D), lambda i: (i, 0)), mod_spec, mod_spec, pl.BlockSpec((1, D), const2)]
                + [pl.BlockSpec(w.shape, const2) for w in wts]
                + [tab_spec] * 6
                + [pl.BlockSpec(g.shape, const2) for g in gains]
                + [pl.BlockSpec(wm.shape, lambda i: (0, 0, 0)), pl.BlockSpec(gb.shape, const2)])
    widths = [(GMLP_WIDTH, jnp.bfloat16)]
    if emit_v:
        widths.append((GMLP_WIDTH, jnp.float32))
    widths += [(N_HEADS * HEAD_DIM, jnp.bfloat16),
               (N_KV_HEADS * HEAD_DIM, jnp.float32), (N_KV_HEADS * HEAD_DIM, jnp.bfloat16),
               (N_KV_HEADS * HEAD_DIM, jnp.float32), (N_KV_HEADS * HEAD_DIM, jnp.bfloat16),
               (N_IDX_HEADS * IDX_DIM, jnp.bfloat16),
               (LANES, jnp.float32), (LANES, jnp.bfloat16),
               (2 * D, jnp.bfloat16)]
    return pl.pallas_call(
        functools.partial(_inproj_kernel, emit_v=emit_v),
        grid=(nt,),
        in_specs=in_specs,
        out_specs=[pl.BlockSpec((tm, w), lambda i: (i, 0)) for w, _ in widths],
        out_shape=[jax.ShapeDtypeStruct((T, w), dt) for w, dt in widths],
        compiler_params=_cparams(("parallel",)),
        name="inproj_v" if emit_v else "inproj",
    )(x2d, sc, sh, g1, *wts, *tabs, *gains, wm, gb)


def _dsa_kernel(qi_ref, wi_ref, q_ref, ki_ref, k_ref, vt_ref, o_ref,
                keys_ref, bias_ref, meff_ref, acc_ref,
                *, ksel, tile_step, tile_base, lim_lo, lim_hi):
    j = pl.program_id(1)
    L = ki_ref.shape[0]
    ntiles = jnp.minimum(j * tile_step + tile_base, L // KTILE)
    lane = lax.broadcasted_iota(jnp.int32, (1, QBLK), 1)
    limit = j * (KTILE * tile_step) + jnp.where(lane < CHUNK, lim_lo, lim_hi)
    row = lax.broadcasted_iota(jnp.int32, (KTILE, QBLK), 0)

    def tile_off(t):
        return pl.multiple_of(t * KTILE, KTILE)

    wi = wi_ref[...]
    qis = [qi_ref[:, h * IDX_DIM:(h + 1) * IDX_DIM] for h in range(N_IDX_HEADS)]

    def idx_body(t, carry):
        off = tile_off(t)
        kt = ki_ref[pl.ds(off, KTILE), :][:, :IDX_DIM]
        acc = jnp.zeros((KTILE, QBLK), jnp.float32)
        for h in range(N_IDX_HEADS):
            s = _dot_nt(kt, qis[h])
            acc = acc + jnp.maximum(s * (IDX_DIM ** -0.5), 0.0) * wi[h:h + 1, :]
        bits = pltpu.bitcast(acc, jnp.int32)
        key = jnp.where(bits >= 0, bits, bits ^ 0x7FFFFFFF)
        keys_ref[pl.ds(off, KTILE), :] = jnp.where(row + off < limit, key, INT_MIN)
        return carry

    lax.fori_loop(0, ntiles, idx_body, 0)

    def count(indicator):
        def body(t, acc):
            off = tile_off(t)
            ones = indicator(keys_ref[pl.ds(off, KTILE), :], off)
            return acc + ones.reshape(KTILE // SUBLANES, SUBLANES, QBLK).sum(axis=0)
        acc = lax.fori_loop(0, ntiles, body, jnp.zeros((SUBLANES, QBLK), jnp.int32))
        return acc.sum(axis=0, keepdims=True)

    def bit_body(i, tu):
        cand_u = tu | lax.shift_left(jnp.int32(1), 31 - i)
        cand_s = cand_u ^ INT_MIN
        cnt = count(lambda key, off: jnp.where(key >= cand_s, 1, 0))
        return jnp.where(cnt >= ksel, cand_u, tu)

    tu = lax.fori_loop(0, 32, bit_body, jnp.zeros((1, QBLK), jnp.int32))
    thr = jnp.maximum(tu ^ INT_MIN, INT_MIN + 1)

    need = count(lambda key, off: jnp.where(key >= thr, 1, 0)) > ksel
    meff_ref[...] = jnp.full(meff_ref.shape, IDX_BIG, jnp.int32)

    @pl.when(jnp.max(need.astype(jnp.int32)) > 0)
    def _():
        want = ksel - count(lambda key, off: jnp.where(key > thr, 1, 0))
        nbits = int(L).bit_length()

        def m_body(i, mp):
            cand = mp | lax.shift_left(jnp.int32(1), nbits - 1 - i)
            f = count(lambda key, off: jnp.where(key == thr, jnp.where(row + off < cand, 1, 0), 0))
            return jnp.where(f < want, cand, mp)

        mp = lax.fori_loop(0, nbits, m_body, jnp.zeros((1, QBLK), jnp.int32))
        meff_ref[...] = jnp.broadcast_to(jnp.where(need, mp, IDX_BIG), meff_ref.shape)

    meff = meff_ref[0:1, :]

    def bias_body(t, carry):
        off = tile_off(t)
        key = keys_ref[pl.ds(off, KTILE), :]
        tie = jnp.where(row + off <= meff, 0.0, NEG_BIG)
        bias_ref[pl.ds(off, KTILE), :] = jnp.where(key == thr, tie, jnp.where(key > thr, 0.0, NEG_BIG))
        return carry

    lax.fori_loop(0, ntiles, bias_body, 0)

    scale = HEAD_DIM ** -0.5
    qgs = [jnp.concatenate([q_ref[:, (g * Q_PER_KV + h) * HEAD_DIM:(g * Q_PER_KV + h + 1) * HEAD_DIM]
                            for h in range(Q_PER_KV)], axis=0) for g in range(N_KV_HEADS)]
    acc_ref[...] = jnp.zeros(acc_ref.shape, jnp.float32)
    W = Q_PER_KV * QBLK

    def att_body(t, carry):
        off = tile_off(t)
        b = bias_ref[pl.ds(off, KTILE), :]
        b4 = jnp.concatenate([b] * Q_PER_KV, axis=1)
        out = []
        for g in range(N_KV_HEADS):
            m_old, l_old = carry[2 * g], carry[2 * g + 1]
            kt = k_ref[pl.ds(off, KTILE), g * HEAD_DIM:(g + 1) * HEAD_DIM]
            s = _dot_nt(kt, qgs[g]) * scale + b4
            m_new = jnp.maximum(m_old, jnp.max(s, axis=0, keepdims=True))
            alpha = jnp.exp(m_old - m_new)
            p = jnp.exp(s - m_new)
            l_new = alpha * l_old + jnp.sum(p, axis=0, keepdims=True)
            vt = vt_ref[g * HEAD_DIM:(g + 1) * HEAD_DIM, pl.ds(off, KTILE)]
            acc_ref[g] = acc_ref[g] * alpha + _bdot(vt, p.astype(jnp.bfloat16))
            out += [m_new, l_new]
        return tuple(out)

    init = (jnp.full((1, W), NEG_BIG, jnp.float32), jnp.zeros((1, W), jnp.float32)) * N_KV_HEADS
    fin = lax.fori_loop(0, ntiles, att_body, init)
    for g in range(N_KV_HEADS):
        o = acc_ref[g] * (1.0 / fin[2 * g + 1])
        for h in range(Q_PER_KV):
            hd = g * Q_PER_KV + h
            o_ref[:, hd * HEAD_DIM:(hd + 1) * HEAD_DIM] = o[:, h * QBLK:(h + 1) * QBLK].T.astype(jnp.bfloat16)


def _dsa(qi, wit, q, ki16, k16, vt, *, ksel, tile_step, tile_base, lim_lo, lim_hi):
    NB, L, _ = ki16.shape
    Sq = wit.shape[2]
    nq = Sq // QBLK
    kern = functools.partial(_dsa_kernel, ksel=ksel, tile_step=tile_step, tile_base=tile_base,
                             lim_lo=lim_lo, lim_hi=lim_hi)
    return pl.pallas_call(
        kern,
        grid=(NB, nq),
        in_specs=[pl.BlockSpec((QBLK, qi.shape[1]), lambda b, j: (b * nq + j, 0)),
                  pl.BlockSpec((None, N_IDX_HEADS, QBLK), lambda b, j: (b, 0, j)),
                  pl.BlockSpec((QBLK, q.shape[1]), lambda b, j: (b * nq + j, 0)),
                  pl.BlockSpec((None, L, LANES), lambda b, j: (b, 0, 0)),
                  pl.BlockSpec((None, L, k16.shape[2]), lambda b, j: (b, 0, 0)),
                  pl.BlockSpec((None, vt.shape[1], L), lambda b, j: (b, 0, 0))],
        out_specs=pl.BlockSpec((QBLK, q.shape[1]), lambda b, j: (b * nq + j, 0)),
        out_shape=jax.ShapeDtypeStruct(q.shape, jnp.bfloat16),
        scratch_shapes=[pltpu.VMEM((L, QBLK), jnp.int32),
                        pltpu.VMEM((L, QBLK), jnp.float32),
                        pltpu.VMEM((SUBLANES, QBLK), jnp.int32),
                        pltpu.VMEM((N_KV_HEADS, HEAD_DIM, Q_PER_KV * QBLK), jnp.float32)],
        compiler_params=_cparams(("parallel", "arbitrary")),
        name="dsa",
    )(qi, wit, q, ki16, k16, vt)


def _merge_kernel(a_ref, b_ref, sg_ref, x_ref, gt_ref, sc2_ref, sh2_ref, g2_ref,
                  wa_ref, wb_ref, wo_ref, wq_ref, xm_ref, h2_ref, qp_ref):
    D = x_ref.shape[1]
    sg = sg_ref[...].astype(jnp.float32)
    merged = sg[:, :D] * _bdot(a_ref[...], wa_ref[...]) + sg[:, D:] * _bdot(b_ref[...], wb_ref[...])
    y = _bdot(merged.astype(jnp.bfloat16), wo_ref[...])
    xm = x_ref[...] + gt_ref[...] * y
    xm_ref[...] = xm
    h2 = _modulate(xm, g2_ref[...], sc2_ref[...], sh2_ref[...])
    h2_ref[...] = h2
    qp_ref[...] = _bdot(h2.astype(jnp.bfloat16), wq_ref[...]).astype(jnp.bfloat16)


def _merge(a, b, sg, x2d, gt, sc2, sh2, g2, wa, wb, wo, wq, *, tm):
    T, D = x2d.shape
    nt = T // tm
    n_mod, R, _ = gt.shape
    tiles_per_mod = nt // n_mod
    const2 = lambda i: (0, 0)
    row = lambda w: pl.BlockSpec((tm, w), lambda i: (i, 0))
    mod_spec = pl.BlockSpec((None, R, D), lambda i: (i // tiles_per_mod, 0, 0))
    nq = wq.shape[1]
    return pl.pallas_call(
        _merge_kernel,
        grid=(nt,),
        in_specs=[row(a.shape[1]), row(b.shape[1]), row(sg.shape[1]), row(D), mod_spec, mod_spec, mod_spec,
                  pl.BlockSpec((1, D), const2)] + [pl.BlockSpec(w.shape, const2) for w in (wa, wb, wo, wq)],
        out_specs=[row(D), row(D), row(nq)],
        out_shape=[jax.ShapeDtypeStruct((T, D), jnp.float32), jax.ShapeDtypeStruct((T, D), jnp.float32),
                   jax.ShapeDtypeStruct((T, nq), jnp.bfloat16)],
        compiler_params=_cparams(("parallel",)),
        name="merge",
    )(a, b, sg, x2d, gt, sc2, sh2, g2, wa, wb, wo, wq)


def _route_kernel(qp_ref, sk_ref, e_ref, g_ref, s_ref, sv_ref, si_ref, gv_ref, ci_ref):
    TT = qp_ref.shape[0]
    half = PEER_KEY_DIM // 2

    def extract(R, vals_ref, idx_ref):
        rowi = lax.broadcasted_iota(jnp.int32, (R, TT), 0)

        def body(r, carry):
            s = s_ref[0:R, :]
            m = jnp.max(s, axis=0, keepdims=True)
            am = jnp.min(jnp.where(s == m, rowi, R), axis=0, keepdims=True)
            vals_ref[pl.ds(r, 1), :] = m
            idx_ref[pl.ds(r, 1), :] = am
            s_ref[0:R, :] = jnp.where(rowi == am, -jnp.inf, s)
            return carry

        lax.fori_loop(0, PEER_TOPK, body, 0)

    for p in range(2):
        s_ref[0:PEER_N_KEYS, :] = _dot_nt(sk_ref[p], qp_ref[:, p * half:(p + 1) * half])
        extract(PEER_N_KEYS, sv_ref.at[p], si_ref.at[p])
    sv0, sv1 = sv_ref[0], sv_ref[1]
    for a in range(PEER_TOPK):
        s_ref[a * PEER_TOPK:(a + 1) * PEER_TOPK, :] = sv0[a:a + 1, :] + sv1
    extract(PEER_TOPK * PEER_TOPK, gv_ref, ci_ref)
    ci, gv = ci_ref[...], gv_ref[...]
    ca, cb = ci // PEER_TOPK, ci % PEER_TOPK
    si0, si1 = si_ref[0], si_ref[1]
    i1 = jnp.zeros_like(ci)
    i2 = jnp.zeros_like(ci)
    for a in range(PEER_TOPK):
        i1 = i1 + jnp.where(ca == a, si0[a:a + 1, :], 0)
        i2 = i2 + jnp.where(cb == a, si1[a:a + 1, :], 0)
    e_ref[...] = i1 * PEER_N_KEYS + i2
    ex = jnp.exp(gv - jnp.max(gv, axis=0, keepdims=True))
    g_ref[...] = ex / jnp.sum(ex, axis=0, keepdims=True)


def _route(qp, sk16, *, tt):
    T = qp.shape[0]
    nt = T // tt
    out = jax.ShapeDtypeStruct((nt, PEER_HEADS, PEER_TOPK, tt), jnp.int32)
    blk = pl.BlockSpec((None, None, PEER_TOPK, tt), lambda i, h: (i, h, 0, 0))
    return pl.pallas_call(
        _route_kernel,
        grid=(nt, PEER_HEADS),
        in_specs=[pl.BlockSpec((tt, PEER_KEY_DIM), lambda i, h: (i, h)),
                  pl.BlockSpec((None, 2, PEER_N_KEYS, PEER_KEY_DIM // 2), lambda i, h: (h, 0, 0, 0))],
        out_specs=[blk, blk],
        out_shape=[out, jax.ShapeDtypeStruct(out.shape, jnp.float32)],
        scratch_shapes=[pltpu.VMEM((PEER_TOPK * PEER_TOPK, tt), jnp.float32),
                        pltpu.VMEM((2, PEER_TOPK, tt), jnp.float32),
                        pltpu.VMEM((2, PEER_TOPK, tt), jnp.int32),
                        pltpu.VMEM((PEER_TOPK, tt), jnp.float32),
                        pltpu.VMEM((PEER_TOPK, tt), jnp.int32)],
        compiler_params=_cparams(("parallel", "arbitrary")),
        name="peer_route",
    )(qp, sk16)


def _peer_kernel(ids_ref, g_ref, h2_ref, xm_ref, gt_ref, u_hbm, v_hbm, o_ref, ubuf, vbuf, sem):
    TP = h2_ref.shape[0]
    NE = g_ref.shape[1]

    def row_copies(t, k, slot):
        e = ids_ref[t, k]
        return (pltpu.make_async_copy(u_hbm.at[pl.ds(e, 1)], ubuf.at[slot, pl.ds(k, 1)], sem.at[0, slot]),
                pltpu.make_async_copy(v_hbm.at[pl.ds(e, 1)], vbuf.at[slot, pl.ds(k, 1)], sem.at[1, slot]))

    def issue(t, slot):
        def body(k, carry):
            cu, cv = row_copies(t, k, slot)
            cu.start()
            cv.start()
            return carry
        lax.fori_loop(0, NE, body, 0, unroll=8)

    def wait(t, slot):
        def body(k, carry):
            cu, cv = row_copies(t, k, slot)
            cu.wait()
            cv.wait()
            return carry
        lax.fori_loop(0, NE, body, 0, unroll=8)

    issue(0, 0)

    def tok(t, carry):
        slot = lax.rem(t, 2)

        @pl.when(t + 1 < TP)
        def _():
            issue(t + 1, 1 - slot)

        wait(t, slot)
        x = jnp.broadcast_to(h2_ref[pl.ds(t, 1), :], (SUBLANES, h2_ref.shape[1])).astype(jnp.bfloat16)
        act = _dot_nt(x, ubuf[slot].astype(jnp.bfloat16))[0:1, :]
        c = g_ref[pl.ds(t, 1), :] * jax.nn.gelu(act)
        c8 = jnp.broadcast_to(c, (SUBLANES, NE)).astype(jnp.bfloat16)
        y = _bdot(c8, vbuf[slot].astype(jnp.bfloat16))[0:1, :]
        gt = gt_ref[pl.ds(t, 1), :] if gt_ref.shape[0] == TP else gt_ref[...]
        o_ref[pl.ds(t, 1), :] = xm_ref[pl.ds(t, 1), :] + gt * y
        return carry

    lax.fori_loop(0, TP, tok, 0)


def _peer(ids, g, h2, xm, gt, u_tab, v_tab):
    T, D = h2.shape
    NE = ids.shape[1]
    nt = T // PEER_TOK
    n_mod, R, _ = gt.shape
    tiles_per_mod = nt // n_mod
    row = lambda w: pl.BlockSpec((PEER_TOK, w), lambda i: (i, 0))
    return pl.pallas_call(
        _peer_kernel,
        grid=(nt,),
        in_specs=[pl.BlockSpec((PEER_TOK, NE), lambda i: (i, 0), memory_space=pltpu.SMEM),
                  row(NE), row(D), row(D),
                  pl.BlockSpec((None, R, D), lambda i: (i // tiles_per_mod, 0, 0)),
                  pl.BlockSpec(memory_space=pl.ANY), pl.BlockSpec(memory_space=pl.ANY)],
        out_specs=row(D),
        out_shape=jax.ShapeDtypeStruct((T, D), jnp.float32),
        scratch_shapes=[pltpu.VMEM((2, NE, D), jnp.float32), pltpu.VMEM((2, NE, D), jnp.float32),
                        pltpu.SemaphoreType.DMA((2, 2))],
        compiler_params=_cparams(("arbitrary",)),
        name="peer_gather",
    )(ids, g, h2, xm, gt, u_tab, v_tab)


def _rope_tables(pos, rot_dim, width):
    half = rot_dim // 2
    inv_freq = ROPE_THETA ** (-jnp.arange(half, dtype=jnp.float32) / half)
    ang = pos.astype(jnp.float32)[..., None] * inv_freq
    cos, sin = jnp.cos(ang), jnp.sin(ang)
    w = np.arange(LANES) % width
    first, second = w < half, (w >= half) & (w < rot_dim)
    src = np.where(first, w, np.where(second, w - half, 0))
    c = jnp.where(first | second, cos[:, src], 1.0)
    sa = jnp.where(first, -sin[:, src], 0.0)
    sb = jnp.where(second, sin[:, src], 0.0)
    return c, sa, sb


def _gmlp_mask():
    i = np.arange(GMLP_CHUNK)
    return (i[None, :] // CHUNK) <= (i[:, None] // CHUNK)


def _rows_per_seq(v, reps, tm):
    nb, d = v.shape
    return jnp.repeat(v, reps, axis=0).reshape(nb * reps // tm, tm, d)


def _layer(xp, xs, cache_k, cache_v, cache_kidx, c_prompt, c_sample, ada_w, ada_b, norm1_g, norm2_g, w_in,
           q_norm_g, k_norm_g, kidx_norm_g, gmlp_v_norm_g, gmlp_ws, gmlp_b, w_branch_a, w_branch_b, w_out,
           peer_wq, peer_subkeys, peer_u, peer_v):
    B, S, D = xp.shape
    DB, DS, _ = xs.shape
    P = cache_k.shape[1]
    Tp, Ts = B * S, DB * DS
    tm_p, tm_s = 2 * GMLP_CHUNK, GMLP_CHUNK
    assert S % tm_p == 0 and GMLP_CHUNK % DS == 0 and Ts % tm_s == 0 and DS <= CHUNK and P % CHUNK == 0
    assert Tp % PEER_TOK == 0 and Ts % PEER_TOK == 0 and S % PEER_TOK == 0
    bf = jnp.bfloat16
    f32 = jnp.float32

    nrow = B + DB
    npad = -nrow % SUBLANES
    c_all = jnp.concatenate([c_prompt, c_sample, jnp.zeros((npad, D), f32)], axis=0)
    mod = _adaln(c_all, ada_w, ada_b)
    mods = jnp.split(mod, 6, axis=-1)
    mp = [m[:B].reshape(B, 1, D) for m in mods]
    ms = [_rows_per_seq(m[B:B + DB], DS, tm_s) for m in mods]

    sizes = (GMLP_WIDTH, GMLP_WIDTH, N_HEADS * HEAD_DIM, N_KV_HEADS * HEAD_DIM, N_KV_HEADS * HEAD_DIM,
             N_IDX_HEADS * IDX_DIM, IDX_DIM, N_IDX_HEADS, 2 * D)
    pts = [int(s) for s in np.cumsum(sizes)[:-1]]
    wu, wv, wq, wk, wva, wqi, wki, wwi, wg = jnp.split(w_in.astype(bf), pts, axis=-1)
    wkw = jnp.concatenate([wki, wwi, jnp.zeros((D, LANES - IDX_DIM - N_IDX_HEADS), bf)], axis=-1)
    wts = (wu, wv, wq, wk, wva, wqi, wg, wkw)
    kig = jnp.concatenate([kidx_norm_g, jnp.zeros((LANES - IDX_DIM,), f32)]).reshape(1, LANES)
    gains = (q_norm_g.reshape(1, HEAD_DIM), k_norm_g.reshape(1, HEAD_DIM), kig, gmlp_v_norm_g.reshape(1, GMLP_WIDTH))
    g1 = norm1_g.reshape(1, D)
    g2 = norm2_g.reshape(1, D)

    wmask = jnp.where(_gmlp_mask()[None], gmlp_ws, 0.0)
    wm_p = wmask.astype(bf)
    gb_p = jnp.repeat(gmlp_b.T, GMLP_GROUP_DIM, axis=1)
    reps = GMLP_CHUNK // DS
    eye = jnp.eye(reps, dtype=f32)
    wm_s = jnp.einsum("ab,gij->gaibj", eye, wmask[:, :DS, :DS]).reshape(GMLP_GROUPS, GMLP_CHUNK, GMLP_CHUNK).astype(bf)
    gb_s = jnp.tile(gb_p[:DS], (reps, 1))

    pos_p = jnp.arange(S)
    pos_s = P + (jnp.arange(tm_s) % DS)
    tabs_p = _rope_tables(pos_p, ROT_DIM, HEAD_DIM) + _rope_tables(pos_p, IDX_ROT_DIM, IDX_DIM)
    tabs_s = _rope_tables(pos_s, ROT_DIM, HEAD_DIM) + _rope_tables(pos_s, IDX_ROT_DIM, IDX_DIM)

    x2p = xp.reshape(Tp, D)
    x2s = xs.reshape(Ts, D)

    (a_p, q_p, k32_p, k16_p, va32_p, va16_p, qi_p, kw32_p, ki16_p, sg_p) = _inproj(
        x2p, mp[1], mp[0], g1, wts, tabs_p, gains, wm_p, gb_p, tm=tm_p, tab_blocks=S // tm_p, emit_v=False)
    wit_p = kw32_p[:, IDX_DIM:IDX_DIM + N_IDX_HEADS].reshape(B, S, N_IDX_HEADS).transpose(0, 2, 1)
    vt_p = va16_p.reshape(B, S, -1).transpose(0, 2, 1)
    b_p = _dsa(qi_p, wit_p, q_p, ki16_p.reshape(B, S, LANES), k16_p.reshape(B, S, -1), vt_p,
               ksel=min(TOPK_MAX, S // 4), tile_step=1, tile_base=1, lim_lo=CHUNK, lim_hi=2 * CHUNK)

    (a_s, v_s, q_s, k32_s, k16_s, va32_s, va16_s, qi_s, kw32_s, ki16_s, sg_s) = _inproj(
        x2s, ms[1], ms[0], g1, wts, tabs_s, gains, wm_s, gb_s, tm=tm_s, tab_blocks=1, emit_v=True)
    Lk = P + DS
    Lpad = -(-Lk // KTILE) * KTILE

    def pad_q(a):
        w = a.shape[-1]
        return jnp.pad(a.reshape(DB, DS, w), ((0, 0), (0, QBLK - DS), (0, 0))).reshape(DB * QBLK, w)

    def cat_keys(cache, new, w):
        return jnp.concatenate([cache.reshape(DB, P, -1).astype(bf), new.reshape(DB, DS, -1),
                                jnp.zeros((DB, Lpad - Lk, w), bf)], axis=1)

    wit_s = jnp.pad(kw32_s[:, IDX_DIM:IDX_DIM + N_IDX_HEADS].reshape(DB, DS, N_IDX_HEADS).transpose(0, 2, 1),
                    ((0, 0), (0, 0), (0, QBLK - DS)))
    kidx_cache = jnp.pad(cache_kidx, ((0, 0), (0, 0), (0, LANES - IDX_DIM)))
    ki_all = cat_keys(kidx_cache, ki16_s, LANES)
    k_all = cat_keys(cache_k, k16_s, N_KV_HEADS * HEAD_DIM)
    vt_all = cat_keys(cache_v, va16_s, N_KV_HEADS * HEAD_DIM).transpose(0, 2, 1)
    b_s = _dsa(pad_q(qi_s), wit_s, pad_q(q_s), ki_all, k_all, vt_all,
               ksel=min(TOPK_MAX, Lk // 4), tile_step=0, tile_base=Lpad // KTILE, lim_lo=Lk, lim_hi=Lk)
    b_s = b_s.reshape(DB, QBLK, -1)[:, :DS].reshape(Ts, -1)

    wa, wb, wo, wpq = w_branch_a.astype(bf), w_branch_b.astype(bf), w_out.astype(bf), peer_wq.astype(bf)
    sk16 = peer_subkeys.astype(bf)

    def tail(a, b, sg, x2d, m, tm):
        xm, h2, qp = _merge(a, b, sg, x2d, m[2], m[4], m[3], g2, wa, wb, wo, wpq, tm=tm)
        e, g = _route(qp, sk16, tt=LANES)
        T = x2d.shape[0]
        ids = e.transpose(0, 3, 1, 2).reshape(T, PEER_HEADS * PEER_TOPK)
        gw = g.transpose(0, 3, 1, 2).reshape(T, PEER_HEADS * PEER_TOPK)
        return ids, gw, h2, xm

    ids_p, gw_p, h2_p, xm_p = tail(a_p, b_p, sg_p, x2p, mp, tm_p)
    out_p = _peer(ids_p, gw_p, h2_p, xm_p, mp[5], peer_u, peer_v)
    ms_peer = [_rows_per_seq(m[B:B + DB], DS, PEER_TOK) for m in mods]
    ids_s, gw_s, h2_s, xm_s = tail(a_s, b_s, sg_s, x2s, ms, tm_s)
    out_s = _peer(ids_s, gw_s, h2_s, xm_s, ms_peer[5], peer_u, peer_v)

    new = (k32_p.reshape(B, S, N_KV_HEADS, HEAD_DIM), va32_p.reshape(B, S, N_KV_HEADS, HEAD_DIM),
           kw32_p[:, :IDX_DIM].reshape(B, S, IDX_DIM),
           k32_s.reshape(DB, DS, N_KV_HEADS, HEAD_DIM), va32_s.reshape(DB, DS, N_KV_HEADS, HEAD_DIM),
           kw32_s[:, :IDX_DIM].reshape(DB, DS, IDX_DIM), v_s.reshape(DB, DS, GMLP_WIDTH))
    return out_p.reshape(B, S, D), out_s.reshape(DB, DS, D), new


def kernel(x_prompt, x_sample, cache_k, cache_v, cache_kidx, c_prompt, c_sample, ada_w, ada_b, norm1_g, norm2_g,
           w_in, q_norm_g, k_norm_g, kidx_norm_g, gmlp_v_norm_g, gmlp_ws, gmlp_b, w_branch_a, w_branch_b, w_out,
           peer_wq, peer_subkeys, peer_u, peer_v):
    xp, xs = x_prompt, x_sample
    per_layer = []
    for l in range(ada_w.shape[0]):
        xp, xs, new = _layer(xp, xs, cache_k[l], cache_v[l], cache_kidx[l], c_prompt, c_sample, ada_w[l], ada_b[l],
                             norm1_g[l], norm2_g[l], w_in[l], q_norm_g[l], k_norm_g[l], kidx_norm_g[l],
                             gmlp_v_norm_g[l], gmlp_ws[l], gmlp_b[l], w_branch_a[l], w_branch_b[l], w_out[l],
                             peer_wq[l], peer_subkeys[l], peer_u[l], peer_v[l])
        per_layer.append(new)
    stacked = tuple(jnp.stack([n[i] for n in per_layer]) for i in range(7))
    return (xp, xs) + stacked
```

```python
import functools

import jax
import jax.numpy as jnp
import numpy as np
from jax import lax
from jax.experimental import pallas as pl
from jax.experimental.pallas import tpu as pltpu

CHUNK = 64
EPS = 1e-6
ROPE_THETA = 500000.0
GMLP_CHUNK = 128
GMLP_GROUPS = 8
GMLP_GROUP_DIM = 128
GMLP_WIDTH = GMLP_GROUPS * GMLP_GROUP_DIM
N_HEADS = 8
N_KV_HEADS = 2
Q_PER_KV = N_HEADS // N_KV_HEADS
HEAD_DIM = 128
ROT_DIM = HEAD_DIM // 4
N_IDX_HEADS = 8
IDX_DIM = 64
IDX_ROT_DIM = IDX_DIM // 4
TOPK_MAX = 256
PEER_HEADS = 8
PEER_N_KEYS = 128
PEER_KEY_DIM = 256
PEER_TOPK = 16

LANES = 128
SUBLANES = 8
VMEM_LIMIT_BYTES = 56 * 1024 * 1024

QBLK = LANES
KT_PROMPT = 512
KT_SAMPLE = 256
QK_FOLD = HEAD_DIM ** -0.5 * float(np.log2(np.e))
PEER_TOK = 128
PEER_GROUP = 4
PEER_ROWS = 8
INT_MIN = -(2 ** 31)
NEG_BIG = -1e30
IDX_BIG = 2 ** 30


def _cparams(sem):
    return pltpu.CompilerParams(dimension_semantics=sem, vmem_limit_bytes=VMEM_LIMIT_BYTES)


def _rms(x, width):
    return x * lax.rsqrt(jnp.sum(x * x, axis=-1, keepdims=True) * (1.0 / width) + EPS)


def _modulate(x, g, sc, sh):
    return _rms(x, x.shape[-1]) * g * (1.0 + sc) + sh


def _rope(x, c, sa, sb, half):
    n = x.shape[-1]
    return x * c + pltpu.roll(x, n - half, 1) * sa + pltpu.roll(x, half, 1) * sb


def _bdot(a, b):
    return jnp.dot(a, b, preferred_element_type=jnp.float32)


def _dot_nt(a, b):
    return lax.dot_general(a, b, (((1,), (1,)), ((), ())), preferred_element_type=jnp.float32)


def _adaln_kernel(c_ref, w_ref, b_ref, o_ref):
    c = c_ref[...]
    h = (c * jax.nn.sigmoid(c)).astype(jnp.bfloat16)
    o_ref[...] = _bdot(h, w_ref[...].astype(jnp.bfloat16)) + b_ref[...]


def _adaln(c, w, b):
    rows, d = c.shape
    n = w.shape[1]
    tn = n // 4
    return pl.pallas_call(
        _adaln_kernel,
        grid=(n // tn,),
        in_specs=[pl.BlockSpec((rows, d), lambda j: (0, 0)),
                  pl.BlockSpec((d, tn), lambda j: (0, j)),
                  pl.BlockSpec((1, tn), lambda j: (0, j))],
        out_specs=pl.BlockSpec((rows, tn), lambda j: (0, j)),
        out_shape=jax.ShapeDtypeStruct((rows, n), jnp.float32),
        compiler_params=_cparams(("arbitrary",)),
        name="adaln",
    )(c, w, b.reshape(1, n))


def _inproj_kernel(x_ref, sc_ref, sh_ref, g1_ref,
                   wu_ref, wv_ref, wq_ref, wk_ref, wva_ref, wqi_ref, wg_ref, wkw_ref,
                   c128_ref, sa128_ref, sb128_ref, c64_ref, sa64_ref, sb64_ref,
                   qg_ref, kg_ref, kig_ref, gvg_ref, wm_ref, gb_ref,
                   *out_refs, emit_v):
    if emit_v:
        (a_ref, v_ref, q_ref, k32_ref, k16_ref, va32_ref, va16_ref, qi_ref, kw32_ref, ki16_ref, sg_ref) = out_refs
    else:
        (a_ref, q_ref, k32_ref, k16_ref, va32_ref, va16_ref, qi_ref, kw32_ref, ki16_ref, sg_ref) = out_refs
        v_ref = None
    tm = x_ref.shape[0]
    h = _modulate(x_ref[...], g1_ref[...], sc_ref[...], sh_ref[...]).astype(jnp.bfloat16)

    c128, sa128, sb128 = c128_ref[...], sa128_ref[...], sb128_ref[...]
    c64, sa64, sb64 = c64_ref[...], sa64_ref[...], sb64_ref[...]

    v = _rms(jax.nn.gelu(_bdot(h, wv_ref[...])), GMLP_WIDTH) * gvg_ref[...]
    if v_ref is not None:
        v_ref[...] = v
    v16 = v.astype(jnp.bfloat16)
    u = jax.nn.gelu(_bdot(h, wu_ref[...]))
    for c in range(tm // GMLP_CHUNK):
        rows = slice(c * GMLP_CHUNK, (c + 1) * GMLP_CHUNK)
        for g in range(GMLP_GROUPS):
            cols = slice(g * GMLP_GROUP_DIM, (g + 1) * GMLP_GROUP_DIM)
            s = _bdot(wm_ref[g], v16[rows, cols]) + gb_ref[:, cols]
            a_ref[rows, cols] = (u[rows, cols] * s).astype(jnp.bfloat16)

    q = _bdot(h, wq_ref[...])
    for hd in range(N_HEADS):
        cols = slice(hd * HEAD_DIM, (hd + 1) * HEAD_DIM)
        qh = _rope(_rms(q[:, cols], HEAD_DIM) * qg_ref[...], c128, sa128, sb128, ROT_DIM // 2)
        q_ref[:, cols] = (qh * QK_FOLD).astype(jnp.bfloat16)
    k = _bdot(h, wk_ref[...])
    for hd in range(N_KV_HEADS):
        cols = slice(hd * HEAD_DIM, (hd + 1) * HEAD_DIM)
        kh = _rope(_rms(k[:, cols], HEAD_DIM) * kg_ref[...], c128, sa128, sb128, ROT_DIM // 2)
        k32_ref[:, cols] = kh
        k16_ref[:, cols] = kh.astype(jnp.bfloat16)
    va = _bdot(h, wva_ref[...])
    va32_ref[...] = va
    va16_ref[...] = va.astype(jnp.bfloat16)

    qi = _bdot(h, wqi_ref[...])
    for p in range(N_IDX_HEADS * IDX_DIM // LANES):
        cols = slice(p * LANES, (p + 1) * LANES)
        qi_ref[:, cols] = _rope(qi[:, cols], c64, sa64, sb64, IDX_ROT_DIM // 2).astype(jnp.bfloat16)
    kw = _bdot(h, wkw_ref[...])
    is_ki = lax.broadcasted_iota(jnp.int32, kw.shape, 1) < IDX_DIM
    kis = jnp.where(is_ki, kw, 0.0)
    kin = kis * lax.rsqrt(jnp.sum(kis * kis, axis=-1, keepdims=True) * (1.0 / IDX_DIM) + EPS) * kig_ref[...]
    kir = _rope(kin, c64, sa64, sb64, IDX_ROT_DIM // 2)
    kw32_ref[...] = jnp.where(is_ki, kir, kw * (N_IDX_HEADS ** -0.5))
    ki16_ref[...] = jnp.where(is_ki, kir, 0.0).astype(jnp.bfloat16)

    sg_ref[...] = jax.nn.sigmoid(_bdot(h, wg_ref[...])).astype(jnp.bfloat16)


def _inproj(x2d, sc, sh, g1, wts, tabs, gains, wm, gb, *, tm, tab_blocks, emit_v):
    T, D = x2d.shape
    nt = T // tm
    n_mod, R, _ = sc.shape
    tiles_per_mod = nt // n_mod
    const2 = lambda i: (0, 0)
    mod_spec = pl.BlockSpec((None, R, D), lambda i: (i // tiles_per_mod, 0, 0))
    tab_spec = pl.BlockSpec((tm, LANES), lambda i: (i % tab_blocks, 0))
    in_specs = ([pl.BlockSpec((tm, D), lambda i: (i, 0)), mod_spec, mod_spec, pl.BlockSpec((1, D), const2)]
                + [pl.BlockSpec(w.shape, const2) for w in wts]
                + [tab_spec] * 6
                + [pl.BlockSpec(g.shape, const2) for g in gains]
                + [pl.BlockSpec(wm.shape, lambda i: (0, 0, 0)), pl.BlockSpec(gb.shape, const2)])
    widths = [(GMLP_WIDTH, jnp.bfloat16)]
    if emit_v:
        widths.append((GMLP_WIDTH, jnp.float32))
    widths += [(N_HEADS * HEAD_DIM, jnp.bfloat16),
               (N_KV_HEADS * HEAD_DIM, jnp.float32), (N_KV_HEADS * HEAD_DIM, jnp.bfloat16),
               (N_KV_HEADS * HEAD_DIM, jnp.float32), (N_KV_HEADS * HEAD_DIM, jnp.bfloat16),
               (N_IDX_HEADS * IDX_DIM, jnp.bfloat16),
               (LANES, jnp.float32), (LANES, jnp.bfloat16),
               (2 * D, jnp.bfloat16)]
    return pl.pallas_call(
        functools.partial(_inproj_kernel, emit_v=emit_v),
        grid=(nt,),
        in_specs=in_specs,
        out_specs=[pl.BlockSpec((tm, w), lambda i: (i, 0)) for w, _ in widths],
        out_shape=[jax.ShapeDtypeStruct((T, w), dt) for w, dt in widths],
        compiler_params=_cparams(("parallel",)),
        name="inproj_v" if emit_v else "inproj",
    )(x2d, sc, sh, g1, *wts, *tabs, *gains, wm, gb)


def _dsa_kernel(qi_ref, wi_ref, q_ref, ki_ref, k_ref, vt_ref, o_ref,
                keys_ref, bias_ref, meff_ref, acc_ref,
                *, ksel, KT, vis_step, vis_base, lim_lo, lim_hi):
    j = pl.program_id(1)
    L = ki_ref.shape[0]
    ntiles = jnp.minimum((j * vis_step + vis_base + KT - 1) // KT, L // KT)
    lane = lax.broadcasted_iota(jnp.int32, (1, QBLK), 1)
    limit = j * vis_step + jnp.where(lane < CHUNK, lim_lo, lim_hi)
    row = lax.broadcasted_iota(jnp.int32, (KT, QBLK), 0)

    def tile_off(t):
        return pl.multiple_of(t * KT, KT)

    wi = wi_ref[...] * (IDX_DIM ** -0.5)
    qit = qi_ref[...].astype(jnp.float32).T.astype(jnp.bfloat16)
    zpad = jnp.zeros((LANES - IDX_DIM, QBLK), jnp.bfloat16)
    qis = [jnp.concatenate([qit[h * IDX_DIM:(h + 1) * IDX_DIM, :], zpad], axis=0) for h in range(N_IDX_HEADS)]

    def idx_body(t, carry):
        off = tile_off(t)
        kt = ki_ref[pl.ds(off, KT), :]
        acc = jnp.zeros((KT, QBLK), jnp.float32)
        for h in range(N_IDX_HEADS):
            acc = acc + jnp.maximum(_bdot(kt, qis[h]), 0.0) * wi[h:h + 1, :]
        bits = pltpu.bitcast(acc, jnp.int32)
        key = jnp.where(bits >= 0, bits, bits ^ 0x7FFFFFFF)
        keys_ref[pl.ds(off, KT), :] = jnp.where(row + off < limit, key, INT_MIN)
        return carry

    lax.fori_loop(0, ntiles, idx_body, 0)

    def count(indicator):
        def body(t, acc):
            off = tile_off(t)
            ones = indicator(keys_ref[pl.ds(off, KT), :], off)
            return acc + ones.reshape(KT // SUBLANES, SUBLANES, QBLK).sum(axis=0)
        acc = lax.fori_loop(0, ntiles, body, jnp.zeros((SUBLANES, QBLK), jnp.int32))
        return acc.sum(axis=0, keepdims=True)

    def bit_body(i, tu):
        cand_u = tu | lax.shift_left(jnp.int32(1), 31 - i)
        cand_s = cand_u ^ INT_MIN
        cnt = count(lambda key, off: jnp.where(key >= cand_s, 1, 0))
        return jnp.where(cnt >= ksel, cand_u, tu)

    tu = lax.fori_loop(0, 32, bit_body, jnp.zeros((1, QBLK), jnp.int32))
    thr = jnp.maximum(tu ^ INT_MIN, INT_MIN + 1)

    need = count(lambda key, off: jnp.where(key >= thr, 1, 0)) > ksel
    meff_ref[...] = jnp.full(meff_ref.shape, IDX_BIG, jnp.int32)

    @pl.when(jnp.max(need.astype(jnp.int32)) > 0)
    def _():
        want = ksel - count(lambda key, off: jnp.where(key > thr, 1, 0))
        nbits = int(L).bit_length()

        def m_body(i, mp):
            cand = mp | lax.shift_left(jnp.int32(1), nbits - 1 - i)
            f = count(lambda key, off: jnp.where(key == thr, jnp.where(row + off < cand, 1, 0), 0))
            return jnp.where(f < want, cand, mp)

        mp = lax.fori_loop(0, nbits, m_body, jnp.zeros((1, QBLK), jnp.int32))
        meff_ref[...] = jnp.broadcast_to(jnp.where(need, mp, IDX_BIG), meff_ref.shape)

    meff = meff_ref[0:1, :]

    def bias_body(t, carry):
        off = tile_off(t)
        key = keys_ref[pl.ds(off, KT), :]
        tie = jnp.where(row + off <= meff, 0.0, NEG_BIG)
        bias_ref[pl.ds(off, KT), :] = jnp.where(key == thr, tie, jnp.where(key > thr, 0.0, NEG_BIG))
        return carry

    lax.fori_loop(0, ntiles, bias_body, 0)

    qgs = [jnp.concatenate([q_ref[:, (g * Q_PER_KV + h) * HEAD_DIM:(g * Q_PER_KV + h + 1) * HEAD_DIM]
                            for h in range(Q_PER_KV)], axis=0) for g in range(N_KV_HEADS)]
    acc_ref[...] = jnp.zeros(acc_ref.shape, jnp.float32)
    W = Q_PER_KV * QBLK

    def att_body(t, carry):
        off = tile_off(t)
        b = bias_ref[pl.ds(off, KT), :]
        b4 = jnp.concatenate([b] * Q_PER_KV, axis=1)
        out = []
        for g in range(N_KV_HEADS):
            m_old, l_old = carry[2 * g], carry[2 * g + 1]
            kt = k_ref[pl.ds(off, KT), g * HEAD_DIM:(g + 1) * HEAD_DIM]
            s = _dot_nt(kt, qgs[g]) + b4
            m_new = jnp.maximum(m_old, jnp.max(s, axis=0, keepdims=True))
            alpha = jnp.exp2(m_old - m_new)
            p = jnp.exp2(s - m_new)
            l_new = alpha * l_old + jnp.sum(p, axis=0, keepdims=True)
            vt = vt_ref[g * HEAD_DIM:(g + 1) * HEAD_DIM, pl.ds(off, KT)]
            acc_ref[g] = acc_ref[g] * alpha + _bdot(vt, p.astype(jnp.bfloat16))
            out += [m_new, l_new]
        return tuple(out)

    init = (jnp.full((1, W), NEG_BIG, jnp.float32), jnp.zeros((1, W), jnp.float32)) * N_KV_HEADS
    fin = lax.fori_loop(0, ntiles, att_body, init)
    for g in range(N_KV_HEADS):
        o = acc_ref[g] * (1.0 / fin[2 * g + 1])
        for h in range(Q_PER_KV):
            hd = g * Q_PER_KV + h
            o_ref[:, hd * HEAD_DIM:(hd + 1) * HEAD_DIM] = o[:, h * QBLK:(h + 1) * QBLK].T.astype(jnp.bfloat16)


def _dsa(qi, wit, q, ki16, k16, vt, *, ksel, kt, vis_step, vis_base, lim_lo, lim_hi):
    NB, L, _ = ki16.shape
    Sq = wit.shape[2]
    nq = Sq // QBLK
    assert L % kt == 0
    kern = functools.partial(_dsa_kernel, ksel=ksel, KT=kt, vis_step=vis_step, vis_base=vis_base,
                             lim_lo=lim_lo, lim_hi=lim_hi)
    return pl.pallas_call(
        kern,
        grid=(NB, nq),
        in_specs=[pl.BlockSpec((QBLK, qi.shape[1]), lambda b, j: (b * nq + j, 0)),
                  pl.BlockSpec((None, N_IDX_HEADS, QBLK), lambda b, j: (b, 0, j)),
                  pl.BlockSpec((QBLK, q.shape[1]), lambda b, j: (b * nq + j, 0)),
                  pl.BlockSpec((None, L, LANES), lambda b, j: (b, 0, 0)),
                  pl.BlockSpec((None, L, k16.shape[2]), lambda b, j: (b, 0, 0)),
                  pl.BlockSpec((None, vt.shape[1], L), lambda b, j: (b, 0, 0))],
        out_specs=pl.BlockSpec((QBLK, q.shape[1]), lambda b, j: (b * nq + j, 0)),
        out_shape=jax.ShapeDtypeStruct(q.shape, jnp.bfloat16),
        scratch_shapes=[pltpu.VMEM((L, QBLK), jnp.int32),
                        pltpu.VMEM((L, QBLK), jnp.float32),
                        pltpu.VMEM((SUBLANES, QBLK), jnp.int32),
                        pltpu.VMEM((N_KV_HEADS, HEAD_DIM, Q_PER_KV * QBLK), jnp.float32)],
        compiler_params=_cparams(("parallel", "arbitrary")),
        name="dsa",
    )(qi, wit, q, ki16, k16, vt)


def _merge_kernel(a_ref, b_ref, sg_ref, x_ref, gt_ref, sc2_ref, sh2_ref, g2_ref,
                  wa_ref, wb_ref, wo_ref, wq_ref, xm_ref, h2_ref, qp_ref):
    D = x_ref.shape[1]
    sg = sg_ref[...].astype(jnp.float32)
    merged = sg[:, :D] * _bdot(a_ref[...], wa_ref[...]) + sg[:, D:] * _bdot(b_ref[...], wb_ref[...])
    y = _bdot(merged.astype(jnp.bfloat16), wo_ref[...])
    xm = x_ref[...] + gt_ref[...] * y
    xm_ref[...] = xm
    h2 = _modulate(xm, g2_ref[...], sc2_ref[...], sh2_ref[...])
    h2_ref[...] = h2
    qp_ref[...] = _bdot(h2.astype(jnp.bfloat16), wq_ref[...]).astype(jnp.bfloat16)


def _merge(a, b, sg, x2d, gt, sc2, sh2, g2, wa, wb, wo, wq, *, tm):
    T, D = x2d.shape
    nt = T // tm
    n_mod, R, _ = gt.shape
    tiles_per_mod = nt // n_mod
    const2 = lambda i: (0, 0)
    row = lambda w: pl.BlockSpec((tm, w), lambda i: (i, 0))
    mod_spec = pl.BlockSpec((None, R, D), lambda i: (i // tiles_per_mod, 0, 0))
    nq = wq.shape[1]
    return pl.pallas_call(
        _merge_kernel,
        grid=(nt,),
        in_specs=[row(a.shape[1]), row(b.shape[1]), row(sg.shape[1]), row(D), mod_spec, mod_spec, mod_spec,
                  pl.BlockSpec((1, D), const2)] + [pl.BlockSpec(w.shape, const2) for w in (wa, wb, wo, wq)],
        out_specs=[row(D), row(D), row(nq)],
        out_shape=[jax.ShapeDtypeStruct((T, D), jnp.float32), jax.ShapeDtypeStruct((T, D), jnp.float32),
                   jax.ShapeDtypeStruct((T, nq), jnp.bfloat16)],
        compiler_params=_cparams(("parallel",)),
        name="merge",
    )(a, b, sg, x2d, gt, sc2, sh2, g2, wa, wb, wo, wq)


_PAIR_COUNTS = [PEER_TOPK // (a + 1) for a in range(PEER_TOPK)]
_PAIR_OFFS = [int(v) for v in np.cumsum([0] + _PAIR_COUNTS[:-1])]
_N_PAIRS = sum(_PAIR_COUNTS)
_PAIR_ROWS = -(-_N_PAIRS // SUBLANES) * SUBLANES


def _route_kernel(qp_ref, sk_ref, e_ref, g_ref, s_ref, sv_ref, si_ref, cand_ref, pay_ref, gv_ref):
    TT = qp_ref.shape[0]
    half = PEER_KEY_DIM // 2
    NC = PEER_HEADS * 2
    NEG_INF = -jnp.inf

    for c in range(NC):
        s_ref[c] = _dot_nt(sk_ref[c // 2, c % 2], qp_ref[:, c * half:(c + 1) * half])

    rowk = lax.broadcasted_iota(jnp.int32, (PEER_N_KEYS, TT), 0)

    def key_round(r, carry):
        for c in range(NC):
            s = s_ref[c]
            m = jnp.max(s, axis=0, keepdims=True)
            am = jnp.min(jnp.where(s == m, rowk, PEER_N_KEYS), axis=0, keepdims=True)
            sv_ref[c, pl.ds(r, 1), :] = m
            si_ref[c, pl.ds(r, 1), :] = am
            s_ref[c] = jnp.where(rowk == am, NEG_INF, s)
        return carry

    lax.fori_loop(0, PEER_TOPK, key_round, 0)

    for h in range(PEER_HEADS):
        sv0, sv1 = sv_ref[2 * h], sv_ref[2 * h + 1]
        si0, si1 = si_ref[2 * h], si_ref[2 * h + 1]
        cand_ref[h, _PAIR_ROWS - SUBLANES:_PAIR_ROWS, :] = jnp.full((SUBLANES, TT), NEG_INF, jnp.float32)
        pay_ref[h, _PAIR_ROWS - SUBLANES:_PAIR_ROWS, :] = jnp.zeros((SUBLANES, TT), jnp.int32)
        for a in range(PEER_TOPK):
            n, off = _PAIR_COUNTS[a], _PAIR_OFFS[a]
            cand_ref[h, off:off + n, :] = sv0[a:a + 1, :] + sv1[0:n, :]
            pay_ref[h, off:off + n, :] = si0[a:a + 1, :] * PEER_N_KEYS + si1[0:n, :]

    rowp = lax.broadcasted_iota(jnp.int32, (_PAIR_ROWS, TT), 0)

    def pair_round(r, carry):
        for h in range(PEER_HEADS):
            s = cand_ref[h]
            m = jnp.max(s, axis=0, keepdims=True)
            am = jnp.min(jnp.where(s == m, rowp, _PAIR_ROWS), axis=0, keepdims=True)
            hit = rowp == am
            gv_ref[h, pl.ds(r, 1), :] = m
            e_ref[pl.ds(h * PEER_TOPK + r, 1), :] = jnp.sum(jnp.where(hit, pay_ref[h], 0), axis=0, keepdims=True)
            cand_ref[h] = jnp.where(hit, NEG_INF, s)
        return carry

    lax.fori_loop(0, PEER_TOPK, pair_round, 0)

    for h in range(PEER_HEADS):
        gv = gv_ref[h]
        ex = jnp.exp(gv - jnp.max(gv, axis=0, keepdims=True))
        g_ref[h * PEER_TOPK:(h + 1) * PEER_TOPK, :] = ex / jnp.sum(ex, axis=0, keepdims=True)


def _route(qp, sk16, *, tt):
    T = qp.shape[0]
    nt = T // tt
    ne = PEER_HEADS * PEER_TOPK
    out = jax.ShapeDtypeStruct((nt, ne, tt), jnp.int32)
    blk = pl.BlockSpec((None, ne, tt), lambda i: (i, 0, 0))
    return pl.pallas_call(
        _route_kernel,
        grid=(nt,),
        in_specs=[pl.BlockSpec((tt, qp.shape[1]), lambda i: (i, 0)),
                  pl.BlockSpec(sk16.shape, lambda i: (0, 0, 0, 0))],
        out_specs=[blk, blk],
        out_shape=[out, jax.ShapeDtypeStruct(out.shape, jnp.float32)],
        scratch_shapes=[pltpu.VMEM((PEER_HEADS * 2, PEER_N_KEYS, tt), jnp.float32),
                        pltpu.VMEM((PEER_HEADS * 2, PEER_TOPK, tt), jnp.float32),
                        pltpu.VMEM((PEER_HEADS * 2, PEER_TOPK, tt), jnp.int32),
                        pltpu.VMEM((PEER_HEADS, _PAIR_ROWS, tt), jnp.float32),
                        pltpu.VMEM((PEER_HEADS, _PAIR_ROWS, tt), jnp.int32),
                        pltpu.VMEM((PEER_HEADS, PEER_TOPK, tt), jnp.float32)],
        compiler_params=_cparams(("parallel",)),
        name="peer_route",
    )(qp, sk16)


def _peer_kernel(ids_ref, g_ref, h2_ref, xm_ref, gt_ref, w_hbm, o_ref, wbuf, sem):
    TP, D = h2_ref.shape
    NE = g_ref.shape[1]
    HS = D // 2 // LANES
    NG = TP // PEER_GROUP
    bf = jnp.bfloat16

    def issue(grp, half):
        for p in range(PEER_GROUP):
            base = (grp * PEER_GROUP + p) * NE
            slot = half * PEER_GROUP + p
            for k in range(NE):
                src = pl.multiple_of(ids_ref[base + k], PEER_ROWS)
                pltpu.make_async_copy(w_hbm.at[pl.ds(src, PEER_ROWS)],
                                      wbuf.at[slot, pl.ds(k * PEER_ROWS, PEER_ROWS)], sem.at[slot]).start()

    def wait(slot):
        pltpu.make_async_copy(w_hbm.at[pl.ds(0, NE * PEER_ROWS)], wbuf.at[slot], sem.at[slot]).wait()

    def table(slot, r0):
        lo, hi = [], []
        for s in range(HS):
            w = wbuf[slot, pl.ds(r0 + s, NE, stride=PEER_ROWS), :]
            lo.append(pltpu.bitcast(w << 16, jnp.float32).astype(bf))
            hi.append(pltpu.bitcast(w & jnp.int32(-65536), jnp.float32).astype(bf))
        return jnp.concatenate(lo + hi, axis=1)

    issue(0, 0)

    def pair(i, carry):
        group(2 * i, 0)
        group(2 * i + 1, 1)
        return carry

    def group(grp, half):
        @pl.when(grp + 1 < NG)
        def _():
            issue(grp + 1, 1 - half)

        slots = [half * PEER_GROUP + p for p in range(PEER_GROUP)]
        toks = [grp * PEER_GROUP + p for p in range(PEER_GROUP)]
        for slot in slots:
            wait(slot)
        acts = []
        for t, slot in zip(toks, slots):
            x = jnp.broadcast_to(h2_ref[pl.ds(t, 1), :], (SUBLANES, D)).astype(bf)
            acts.append(_dot_nt(x, table(slot, 0))[0:1, :])
        ys = []
        for t, slot, act in zip(toks, slots, acts):
            c = g_ref[pl.ds(t, 1), :] * jax.nn.gelu(act)
            c8 = jnp.broadcast_to(c, (SUBLANES, NE)).astype(bf)
            ys.append(_bdot(c8, table(slot, HS))[0:1, :])
        for t, y in zip(toks, ys):
            gt = gt_ref[pl.ds(t, 1), :] if gt_ref.shape[0] == TP else gt_ref[...]
            o_ref[pl.ds(t, 1), :] = xm_ref[pl.ds(t, 1), :] + gt * y

    lax.fori_loop(0, NG // 2, pair, 0)


def _peer(ids, g, h2, xm, gt, wtab):
    T, D = h2.shape
    NE = ids.shape[1]
    nt = T // PEER_TOK
    n_mod, R, _ = gt.shape
    tiles_per_mod = nt // n_mod
    row = lambda w: pl.BlockSpec((PEER_TOK, w), lambda i: (i, 0))
    return pl.pallas_call(
        _peer_kernel,
        grid=(nt,),
        in_specs=[pl.BlockSpec((PEER_TOK * NE,), lambda i: (i,), memory_space=pltpu.SMEM),
                  row(NE), row(D), row(D),
                  pl.BlockSpec((None, R, D), lambda i: (i // tiles_per_mod, 0, 0)),
                  pl.BlockSpec(memory_space=pl.ANY)],
        out_specs=row(D),
        out_shape=jax.ShapeDtypeStruct((T, D), jnp.float32),
        scratch_shapes=[pltpu.VMEM((2 * PEER_GROUP, NE * PEER_ROWS, LANES), jnp.int32),
                        pltpu.SemaphoreType.DMA((2 * PEER_GROUP,))],
        compiler_params=_cparams(("arbitrary",)),
        name="peer_gather",
    )(ids.reshape(-1), g, h2, xm, gt, wtab)


def _pack_bf16_pairs(t):
    E, D = t.shape
    b = lax.bitcast_convert_type(t.astype(jnp.bfloat16), jnp.uint16).astype(jnp.uint32)
    w = b[:, :D // 2] | (b[:, D // 2:] << 16)
    return lax.bitcast_convert_type(w, jnp.int32).reshape(E, D // 2 // LANES, LANES)


def _rope_tables(pos, rot_dim, width):
    half = rot_dim // 2
    inv_freq = ROPE_THETA ** (-jnp.arange(half, dtype=jnp.float32) / half)
    ang = pos.astype(jnp.float32)[..., None] * inv_freq
    cos, sin = jnp.cos(ang), jnp.sin(ang)
    w = np.arange(LANES) % width
    first, second = w < half, (w >= half) & (w < rot_dim)
    src = np.where(first, w, np.where(second, w - half, 0))
    c = jnp.where(first | second, cos[:, src], 1.0)
    sa = jnp.where(first, -sin[:, src], 0.0)
    sb = jnp.where(second, sin[:, src], 0.0)
    return c, sa, sb


def _gmlp_mask():
    i = np.arange(GMLP_CHUNK)
    return (i[None, :] // CHUNK) <= (i[:, None] // CHUNK)


def _rows_per_seq(v, reps, tm):
    nb, d = v.shape
    return jnp.repeat(v, reps, axis=0).reshape(nb * reps // tm, tm, d)


def _layer(xp, xs, cache_k, cache_v, cache_kidx, c_prompt, c_sample, ada_w, ada_b, norm1_g, norm2_g, w_in,
           q_norm_g, k_norm_g, kidx_norm_g, gmlp_v_norm_g, gmlp_ws, gmlp_b, w_branch_a, w_branch_b, w_out,
           peer_wq, peer_subkeys, peer_u, peer_v):
    B, S, D = xp.shape
    DB, DS, _ = xs.shape
    P = cache_k.shape[1]
    Tp, Ts = B * S, DB * DS
    tm_p, tm_s = 2 * GMLP_CHUNK, GMLP_CHUNK
    assert S % tm_p == 0 and GMLP_CHUNK % DS == 0 and Ts % tm_s == 0 and DS <= CHUNK and P % CHUNK == 0
    assert Tp % PEER_TOK == 0 and Ts % PEER_TOK == 0 and S % PEER_TOK == 0
    bf = jnp.bfloat16
    f32 = jnp.float32

    nrow = B + DB
    npad = -nrow % SUBLANES
    c_all = jnp.concatenate([c_prompt, c_sample, jnp.zeros((npad, D), f32)], axis=0)
    mod = _adaln(c_all, ada_w, ada_b)
    mods = jnp.split(mod, 6, axis=-1)
    mp = [m[:B].reshape(B, 1, D) for m in mods]
    ms = [_rows_per_seq(m[B:B + DB], DS, tm_s) for m in mods]

    sizes = (GMLP_WIDTH, GMLP_WIDTH, N_HEADS * HEAD_DIM, N_KV_HEADS * HEAD_DIM, N_KV_HEADS * HEAD_DIM,
             N_IDX_HEADS * IDX_DIM, IDX_DIM, N_IDX_HEADS, 2 * D)
    pts = [int(s) for s in np.cumsum(sizes)[:-1]]
    wu, wv, wq, wk, wva, wqi, wki, wwi, wg = jnp.split(w_in.astype(bf), pts, axis=-1)
    wkw = jnp.concatenate([wki, wwi, jnp.zeros((D, LANES - IDX_DIM - N_IDX_HEADS), bf)], axis=-1)
    wts = (wu, wv, wq, wk, wva, wqi, wg, wkw)
    kig = jnp.concatenate([kidx_norm_g, jnp.zeros((LANES - IDX_DIM,), f32)]).reshape(1, LANES)
    gains = (q_norm_g.reshape(1, HEAD_DIM), k_norm_g.reshape(1, HEAD_DIM), kig, gmlp_v_norm_g.reshape(1, GMLP_WIDTH))
    g1 = norm1_g.reshape(1, D)
    g2 = norm2_g.reshape(1, D)

    wmask = jnp.where(_gmlp_mask()[None], gmlp_ws, 0.0)
    wm_p = wmask.astype(bf)
    gb_p = jnp.repeat(gmlp_b.T, GMLP_GROUP_DIM, axis=1)
    reps = GMLP_CHUNK // DS
    eye = jnp.eye(reps, dtype=f32)
    wm_s = jnp.einsum("ab,gij->gaibj", eye, wmask[:, :DS, :DS]).reshape(GMLP_GROUPS, GMLP_CHUNK, GMLP_CHUNK).astype(bf)
    gb_s = jnp.tile(gb_p[:DS], (reps, 1))

    pos_p = jnp.arange(S)
    pos_s = P + (jnp.arange(tm_s) % DS)
    tabs_p = _rope_tables(pos_p, ROT_DIM, HEAD_DIM) + _rope_tables(pos_p, IDX_ROT_DIM, IDX_DIM)
    tabs_s = _rope_tables(pos_s, ROT_DIM, HEAD_DIM) + _rope_tables(pos_s, IDX_ROT_DIM, IDX_DIM)

    x2p = xp.reshape(Tp, D)
    x2s = xs.reshape(Ts, D)

    (a_p, q_p, k32_p, k16_p, va32_p, va16_p, qi_p, kw32_p, ki16_p, sg_p) = _inproj(
        x2p, mp[1], mp[0], g1, wts, tabs_p, gains, wm_p, gb_p, tm=tm_p, tab_blocks=S // tm_p, emit_v=False)
    wit_p = kw32_p[:, IDX_DIM:IDX_DIM + N_IDX_HEADS].reshape(B, S, N_IDX_HEADS).transpose(0, 2, 1)
    vt_p = va16_p.reshape(B, S, -1).transpose(0, 2, 1)
    b_p = _dsa(qi_p, wit_p, q_p, ki16_p.reshape(B, S, LANES), k16_p.reshape(B, S, -1), vt_p,
               ksel=min(TOPK_MAX, S // 4), kt=min(KT_PROMPT, S), vis_step=QBLK, vis_base=QBLK,
               lim_lo=CHUNK, lim_hi=2 * CHUNK)

    (a_s, v_s, q_s, k32_s, k16_s, va32_s, va16_s, qi_s, kw32_s, ki16_s, sg_s) = _inproj(
        x2s, ms[1], ms[0], g1, wts, tabs_s, gains, wm_s, gb_s, tm=tm_s, tab_blocks=1, emit_v=True)
    Lk = P + DS
    Lpad = -(-Lk // KT_SAMPLE) * KT_SAMPLE

    def pad_q(a):
        w = a.shape[-1]
        return jnp.pad(a.reshape(DB, DS, w), ((0, 0), (0, QBLK - DS), (0, 0))).reshape(DB * QBLK, w)

    def cat_keys(cache, new, w):
        return jnp.concatenate([cache.reshape(DB, P, -1).astype(bf), new.reshape(DB, DS, -1),
                                jnp.zeros((DB, Lpad - Lk, w), bf)], axis=1)

    wit_s = jnp.pad(kw32_s[:, IDX_DIM:IDX_DIM + N_IDX_HEADS].reshape(DB, DS, N_IDX_HEADS).transpose(0, 2, 1),
                    ((0, 0), (0, 0), (0, QBLK - DS)))
    kidx_cache = jnp.pad(cache_kidx, ((0, 0), (0, 0), (0, LANES - IDX_DIM)))
    ki_all = cat_keys(kidx_cache, ki16_s, LANES)
    k_all = cat_keys(cache_k, k16_s, N_KV_HEADS * HEAD_DIM)
    vt_all = cat_keys(cache_v, va16_s, N_KV_HEADS * HEAD_DIM).transpose(0, 2, 1)
    b_s = _dsa(pad_q(qi_s), wit_s, pad_q(q_s), ki_all, k_all, vt_all,
               ksel=min(TOPK_MAX, Lk // 4), kt=KT_SAMPLE, vis_step=0, vis_base=Lpad, lim_lo=Lk, lim_hi=Lk)
    b_s = b_s.reshape(DB, QBLK, -1)[:, :DS].reshape(Ts, -1)

    wa, wb, wo, wpq = w_branch_a.astype(bf), w_branch_b.astype(bf), w_out.astype(bf), peer_wq.astype(bf)
    sk16 = peer_subkeys.astype(bf)
    wtab = jnp.concatenate([_pack_bf16_pairs(peer_u), _pack_bf16_pairs(peer_v)], axis=1).reshape(-1, LANES)

    def tail(a, b, sg, x2d, m, tm):
        xm, h2, qp = _merge(a, b, sg, x2d, m[2], m[4], m[3], g2, wa, wb, wo, wpq, tm=tm)
        e, g = _route(qp, sk16, tt=LANES)
        T = x2d.shape[0]
        ids = e.transpose(0, 2, 1).reshape(T, PEER_HEADS * PEER_TOPK) * PEER_ROWS
        gw = g.transpose(0, 2, 1).reshape(T, PEER_HEADS * PEER_TOPK)
        return ids, gw, h2, xm

    ids_p, gw_p, h2_p, xm_p = tail(a_p, b_p, sg_p, x2p, mp, tm_p)
    out_p = _peer(ids_p, gw_p, h2_p, xm_p, mp[5], wtab)
    ms_peer = [_rows_per_seq(m[B:B + DB], DS, PEER_TOK) for m in mods]
    ids_s, gw_s, h2_s, xm_s = tail(a_s, b_s, sg_s, x2s, ms, tm_s)
    out_s = _peer(ids_s, gw_s, h2_s, xm_s, ms_peer[5], wtab)

    new = (k32_p.reshape(B, S, N_KV_HEADS, HEAD_DIM), va32_p.reshape(B, S, N_KV_HEADS, HEAD_DIM),
           kw32_p[:, :IDX_DIM].reshape(B, S, IDX_DIM),
           k32_s.reshape(DB, DS, N_KV_HEADS, HEAD_DIM), va32_s.reshape(DB, DS, N_KV_HEADS, HEAD_DIM),
           kw32_s[:, :IDX_DIM].reshape(DB, DS, IDX_DIM), v_s.reshape(DB, DS, GMLP_WIDTH))
    return out_p.reshape(B, S, D), out_s.reshape(DB, DS, D), new


def kernel(x_prompt, x_sample, cache_k, cache_v, cache_kidx, c_prompt, c_sample, ada_w, ada_b, norm1_g, norm2_g,
           w_in, q_norm_g, k_norm_g, kidx_norm_g, gmlp_v_norm_g, gmlp_ws, gmlp_b, w_branch_a, w_branch_b, w_out,
           peer_wq, peer_subkeys, peer_u, peer_v):
    xp, xs = x_prompt, x_sample
    per_layer = []
    for l in range(ada_w.shape[0]):
        xp, xs, new = _layer(xp, xs, cache_k[l], cache_v[l], cache_kidx[l], c_prompt, c_sample, ada_w[l], ada_b[l],
                             norm1_g[l], norm2_g[l], w_in[l], q_norm_g[l], k_norm_g[l], kidx_norm_g[l],
                             gmlp_v_norm_g[l], gmlp_ws[l], gmlp_b[l], w_branch_a[l], w_branch_b[l], w_out[l],
                             peer_wq[l], peer_subkeys[l], peer_u[l], peer_v[l])
        per_layer.append(new)
    stacked = tuple(jnp.stack([n[i] for n in per_layer]) for i in range(7))
    return (xp, xs) + stacked
```

```python
import functools

import jax
import jax.numpy as jnp
import numpy as np
from jax import lax
from jax.experimental import pallas as pl
from jax.experimental.pallas import tpu as pltpu

CHUNK = 64
EPS = 1e-6
ROPE_THETA = 500000.0
GMLP_CHUNK = 128
GMLP_GROUPS = 8
GMLP_GROUP_DIM = 128
GMLP_WIDTH = GMLP_GROUPS * GMLP_GROUP_DIM
N_HEADS = 8
N_KV_HEADS = 2
Q_PER_KV = N_HEADS // N_KV_HEADS
HEAD_DIM = 128
ROT_DIM = HEAD_DIM // 4
N_IDX_HEADS = 8
IDX_DIM = 64
IDX_ROT_DIM = IDX_DIM // 4
TOPK_MAX = 256
PEER_HEADS = 8
PEER_N_KEYS = 128
PEER_KEY_DIM = 256
PEER_TOPK = 16

LANES = 128
SUBLANES = 8
VMEM_LIMIT_BYTES = 56 * 1024 * 1024

QBLK = LANES
KT_PROMPT = 512
KT_SAMPLE = 256
QK_FOLD = HEAD_DIM ** -0.5 * float(np.log2(np.e))
PEER_TOK = 128
PEER_SETS = 3
PEER_GROUP = 4
PEER_ROWS = 8
INT_MIN = -(2 ** 31)
NEG_BIG = -1e30
IDX_BIG = 2 ** 30


def _cparams(sem):
    return pltpu.CompilerParams(dimension_semantics=sem, vmem_limit_bytes=VMEM_LIMIT_BYTES)


def _rms(x, width):
    return x * lax.rsqrt(jnp.sum(x * x, axis=-1, keepdims=True) * (1.0 / width) + EPS)


def _modulate(x, g, sc, sh):
    return _rms(x, x.shape[-1]) * g * (1.0 + sc) + sh


def _rope(x, c, sa, sb, half):
    n = x.shape[-1]
    return x * c + pltpu.roll(x, n - half, 1) * sa + pltpu.roll(x, half, 1) * sb


def _bdot(a, b):
    return jnp.dot(a, b, preferred_element_type=jnp.float32)


def _dot_nt(a, b):
    return lax.dot_general(a, b, (((1,), (1,)), ((), ())), preferred_element_type=jnp.float32)


def _adaln_kernel(c_ref, w_ref, b_ref, o_ref):
    c = c_ref[...]
    h = (c * jax.nn.sigmoid(c)).astype(jnp.bfloat16)
    o_ref[...] = _bdot(h, w_ref[...].astype(jnp.bfloat16)) + b_ref[...]


def _adaln(c, w, b):
    rows, d = c.shape
    n = w.shape[1]
    tn = n // 4
    return pl.pallas_call(
        _adaln_kernel,
        grid=(n // tn,),
        in_specs=[pl.BlockSpec((rows, d), lambda j: (0, 0)),
                  pl.BlockSpec((d, tn), lambda j: (0, j)),
                  pl.BlockSpec((1, tn), lambda j: (0, j))],
        out_specs=pl.BlockSpec((rows, tn), lambda j: (0, j)),
        out_shape=jax.ShapeDtypeStruct((rows, n), jnp.float32),
        compiler_params=_cparams(("arbitrary",)),
        name="adaln",
    )(c, w, b.reshape(1, n))


def _inproj_kernel(x_ref, sc_ref, sh_ref, g1_ref,
                   wu_ref, wv_ref, wq_ref, wk_ref, wva_ref, wqi_ref, wg_ref, wkw_ref,
                   c128_ref, sa128_ref, sb128_ref, c64_ref, sa64_ref, sb64_ref,
                   qg_ref, kg_ref, kig_ref, gvg_ref, wm_ref, gb_ref,
                   *out_refs, emit_v):
    if emit_v:
        (a_ref, v_ref, q_ref, k32_ref, k16_ref, va32_ref, va16_ref, qi_ref, kw32_ref, ki16_ref, sg_ref) = out_refs
    else:
        (a_ref, q_ref, k32_ref, k16_ref, va32_ref, va16_ref, qi_ref, kw32_ref, ki16_ref, sg_ref) = out_refs
        v_ref = None
    tm = x_ref.shape[0]
    h = _modulate(x_ref[...], g1_ref[...], sc_ref[...], sh_ref[...]).astype(jnp.bfloat16)

    c128, sa128, sb128 = c128_ref[...], sa128_ref[...], sb128_ref[...]
    c64, sa64, sb64 = c64_ref[...], sa64_ref[...], sb64_ref[...]

    v = _rms(jax.nn.gelu(_bdot(h, wv_ref[...])), GMLP_WIDTH) * gvg_ref[...]
    if v_ref is not None:
        v_ref[...] = v
    v16 = v.astype(jnp.bfloat16)
    u = jax.nn.gelu(_bdot(h, wu_ref[...]))
    for c in range(tm // GMLP_CHUNK):
        rows = slice(c * GMLP_CHUNK, (c + 1) * GMLP_CHUNK)
        for g in range(GMLP_GROUPS):
            cols = slice(g * GMLP_GROUP_DIM, (g + 1) * GMLP_GROUP_DIM)
            s = _bdot(wm_ref[g], v16[rows, cols]) + gb_ref[:, cols]
            a_ref[rows, cols] = (u[rows, cols] * s).astype(jnp.bfloat16)

    q = _bdot(h, wq_ref[...])
    for hd in range(N_HEADS):
        cols = slice(hd * HEAD_DIM, (hd + 1) * HEAD_DIM)
        qh = _rope(_rms(q[:, cols], HEAD_DIM) * qg_ref[...], c128, sa128, sb128, ROT_DIM // 2)
        q_ref[:, cols] = (qh * QK_FOLD).astype(jnp.bfloat16)
    k = _bdot(h, wk_ref[...])
    for hd in range(N_KV_HEADS):
        cols = slice(hd * HEAD_DIM, (hd + 1) * HEAD_DIM)
        kh = _rope(_rms(k[:, cols], HEAD_DIM) * kg_ref[...], c128, sa128, sb128, ROT_DIM // 2)
        k32_ref[:, cols] = kh
        k16_ref[:, cols] = kh.astype(jnp.bfloat16)
    va = _bdot(h, wva_ref[...])
    va32_ref[...] = va
    va16_ref[...] = va.astype(jnp.bfloat16)

    qi = _bdot(h, wqi_ref[...])
    for p in range(N_IDX_HEADS * IDX_DIM // LANES):
        cols = slice(p * LANES, (p + 1) * LANES)
        qi_ref[:, cols] = _rope(qi[:, cols], c64, sa64, sb64, IDX_ROT_DIM // 2).astype(jnp.bfloat16)
    kw = _bdot(h, wkw_ref[...])
    is_ki = lax.broadcasted_iota(jnp.int32, kw.shape, 1) < IDX_DIM
    kis = jnp.where(is_ki, kw, 0.0)
    kin = kis * lax.rsqrt(jnp.sum(kis * kis, axis=-1, keepdims=True) * (1.0 / IDX_DIM) + EPS) * kig_ref[...]
    kir = _rope(kin, c64, sa64, sb64, IDX_ROT_DIM // 2)
    kw32_ref[...] = jnp.where(is_ki, kir, kw * (N_IDX_HEADS ** -0.5))
    ki16_ref[...] = jnp.where(is_ki, kir, 0.0).astype(jnp.bfloat16)

    sg_ref[...] = jax.nn.sigmoid(_bdot(h, wg_ref[...])).astype(jnp.bfloat16)


def _inproj(x2d, sc, sh, g1, wts, tabs, gains, wm, gb, *, tm, tab_blocks, emit_v):
    T, D = x2d.shape
    nt = T // tm
    n_mod, R, _ = sc.shape
    tiles_per_mod = nt // n_mod
    const2 = lambda i: (0, 0)
    mod_spec = pl.BlockSpec((None, R, D), lambda i: (i // tiles_per_mod, 0, 0))
    tab_spec = pl.BlockSpec((tm, LANES), lambda i: (i % tab_blocks, 0))
    in_specs = ([pl.BlockSpec((tm, D), lambda i: (i, 0)), mod_spec, mod_spec, pl.BlockSpec((1, D), const2)]
                + [pl.BlockSpec(w.shape, const2) for w in wts]
                + [tab_spec] * 6
                + [pl.BlockSpec(g.shape, const2) for g in gains]
                + [pl.BlockSpec(wm.shape, lambda i: (0, 0, 0)), pl.BlockSpec(gb.shape, const2)])
    widths = [(GMLP_WIDTH, jnp.bfloat16)]
    if emit_v:
        widths.append((GMLP_WIDTH, jnp.float32))
    widths += [(N_HEADS * HEAD_DIM, jnp.bfloat16),
               (N_KV_HEADS * HEAD_DIM, jnp.float32), (N_KV_HEADS * HEAD_DIM, jnp.bfloat16),
               (N_KV_HEADS * HEAD_DIM, jnp.float32), (N_KV_HEADS * HEAD_DIM, jnp.bfloat16),
               (N_IDX_HEADS * IDX_DIM, jnp.bfloat16),
               (LANES, jnp.float32), (LANES, jnp.bfloat16),
               (2 * D, jnp.bfloat16)]
    return pl.pallas_call(
        functools.partial(_inproj_kernel, emit_v=emit_v),
        grid=(nt,),
        in_specs=in_specs,
        out_specs=[pl.BlockSpec((tm, w), lambda i: (i, 0)) for w, _ in widths],
        out_shape=[jax.ShapeDtypeStruct((T, w), dt) for w, dt in widths],
        compiler_params=_cparams(("parallel",)),
        name="inproj_v" if emit_v else "inproj",
    )(x2d, sc, sh, g1, *wts, *tabs, *gains, wm, gb)


def _dsa_kernel(qi_ref, wi_ref, q_ref, ki_ref, k_ref, vt_ref, o_ref,
                keys_ref, bias_ref, half_ref, meff_ref, acc_ref,
                *, ksel, KT, vis_step, vis_base, lim_lo, lim_hi):
    j = pl.program_id(1)
    L = ki_ref.shape[0]
    ntiles = jnp.minimum((j * vis_step + vis_base + KT - 1) // KT, L // KT)
    lane = lax.broadcasted_iota(jnp.int32, (1, QBLK), 1)
    limit = j * vis_step + jnp.where(lane < CHUNK, lim_lo, lim_hi)
    row = lax.broadcasted_iota(jnp.int32, (KT, QBLK), 0)

    def tile_off(t):
        return pl.multiple_of(t * KT, KT)

    wi = wi_ref[...] * (IDX_DIM ** -0.5)
    qit = qi_ref[...].astype(jnp.float32).T.astype(jnp.bfloat16)
    zpad = jnp.zeros((LANES - IDX_DIM, QBLK), jnp.bfloat16)
    qis = [jnp.concatenate([qit[h * IDX_DIM:(h + 1) * IDX_DIM, :], zpad], axis=0) for h in range(N_IDX_HEADS)]

    def idx_body(t, carry):
        off = tile_off(t)
        kt = ki_ref[pl.ds(off, KT), :]
        acc = jnp.zeros((KT, QBLK), jnp.float32)
        for h in range(N_IDX_HEADS):
            acc = acc + jnp.maximum(_bdot(kt, qis[h]), 0.0) * wi[h:h + 1, :]
        bits = pltpu.bitcast(acc, jnp.int32)
        key = jnp.where(bits >= 0, bits, bits ^ 0x7FFFFFFF)
        keys_ref[pl.ds(off, KT), :] = jnp.where(row + off < limit, key, INT_MIN)
        return carry

    lax.fori_loop(0, ntiles, idx_body, 0)

    def count(indicator):
        def body(t, acc):
            off = tile_off(t)
            ones = indicator(keys_ref[pl.ds(off, KT), :], off)
            return acc + ones.reshape(KT // SUBLANES, SUBLANES, QBLK).sum(axis=0)
        acc = lax.fori_loop(0, ntiles, body, jnp.zeros((SUBLANES, QBLK), jnp.int32))
        return acc.sum(axis=0, keepdims=True)

    HALF = 1 << 16
    PK = 2 * SUBLANES

    def count16(cand, strict=False):
        c16 = jnp.broadcast_to(cand, (PK, QBLK)).astype(jnp.int16)[None]

        def body(t, acc):
            x = half_ref[pl.ds(tile_off(t), KT), :].reshape(KT // PK, PK, QBLK)
            hit = (x > c16) if strict else (x >= c16)
            ones = jnp.where(hit, jnp.int16(1), jnp.int16(0))
            parts = [ones[i] for i in range(KT // PK)]
            while len(parts) > 1:
                parts = [a + b for a, b in zip(parts[0::2], parts[1::2])]
            return acc + parts[0]

        acc = lax.fori_loop(0, ntiles, body, jnp.zeros((PK, QBLK), jnp.int16))
        return acc.astype(jnp.int32).sum(axis=0, keepdims=True)

    def search16(base):
        def bit_body(i, tu):
            cand_u = tu | lax.shift_left(jnp.int32(1), 15 - i)
            cnt = base + count16(cand_u - HALF // 2)
            return jnp.where(cnt >= ksel, cand_u, tu)
        return lax.fori_loop(0, 16, bit_body, jnp.zeros((1, QBLK), jnp.int32))

    def split_body(t, carry):
        off = tile_off(t)
        half_ref[pl.ds(off, KT), :] = (keys_ref[pl.ds(off, KT), :] >> 16).astype(jnp.int16)
        return carry

    lax.fori_loop(0, ntiles, split_body, 0)
    tu_hi = search16(0)
    hi_s = tu_hi - HALF // 2
    above = count16(hi_s, strict=True)

    def low_body(t, carry):
        off = tile_off(t)
        key = keys_ref[pl.ds(off, KT), :]
        lo = (key & (HALF - 1)) - HALF // 2
        half_ref[pl.ds(off, KT), :] = jnp.where((key >> 16) == hi_s, lo, -HALF // 2).astype(jnp.int16)
        return carry

    lax.fori_loop(0, ntiles, low_body, 0)
    tu = (tu_hi << 16) | search16(above)
    thr = jnp.maximum(tu ^ INT_MIN, INT_MIN + 1)

    need = count(lambda key, off: jnp.where(key >= thr, 1, 0)) > ksel
    meff_ref[...] = jnp.full(meff_ref.shape, IDX_BIG, jnp.int32)

    @pl.when(jnp.max(need.astype(jnp.int32)) > 0)
    def _():
        want = ksel - count(lambda key, off: jnp.where(key > thr, 1, 0))
        nbits = int(L).bit_length()

        def m_body(i, mp):
            cand = mp | lax.shift_left(jnp.int32(1), nbits - 1 - i)
            f = count(lambda key, off: jnp.where(key == thr, jnp.where(row + off < cand, 1, 0), 0))
            return jnp.where(f < want, cand, mp)

        mp = lax.fori_loop(0, nbits, m_body, jnp.zeros((1, QBLK), jnp.int32))
        meff_ref[...] = jnp.broadcast_to(jnp.where(need, mp, IDX_BIG), meff_ref.shape)

    meff = meff_ref[0:1, :]

    def bias_body(t, carry):
        off = tile_off(t)
        key = keys_ref[pl.ds(off, KT), :]
        tie = jnp.where(row + off <= meff, 0.0, NEG_BIG)
        bias_ref[pl.ds(off, KT), :] = jnp.where(key == thr, tie, jnp.where(key > thr, 0.0, NEG_BIG))
        return carry

    lax.fori_loop(0, ntiles, bias_body, 0)

    qgs = [jnp.concatenate([q_ref[:, (g * Q_PER_KV + h) * HEAD_DIM:(g * Q_PER_KV + h + 1) * HEAD_DIM]
                            for h in range(Q_PER_KV)], axis=0) for g in range(N_KV_HEADS)]
    acc_ref[...] = jnp.zeros(acc_ref.shape, jnp.float32)
    W = Q_PER_KV * QBLK

    def att_body(t, carry):
        off = tile_off(t)
        b = bias_ref[pl.ds(off, KT), :]
        b4 = jnp.concatenate([b] * Q_PER_KV, axis=1)
        out = []
        for g in range(N_KV_HEADS):
            m_old, l_old = carry[2 * g], carry[2 * g + 1]
            kt = k_ref[pl.ds(off, KT), g * HEAD_DIM:(g + 1) * HEAD_DIM]
            s = _dot_nt(kt, qgs[g]) + b4
            m_new = jnp.maximum(m_old, jnp.max(s, axis=0, keepdims=True))
            alpha = jnp.exp2(m_old - m_new)
            p = jnp.exp2(s - m_new)
            l_new = alpha * l_old + jnp.sum(p, axis=0, keepdims=True)
            vt = vt_ref[g * HEAD_DIM:(g + 1) * HEAD_DIM, pl.ds(off, KT)]
            acc_ref[g] = acc_ref[g] * alpha + _bdot(vt, p.astype(jnp.bfloat16))
            out += [m_new, l_new]
        return tuple(out)

    init = (jnp.full((1, W), NEG_BIG, jnp.float32), jnp.zeros((1, W), jnp.float32)) * N_KV_HEADS
    fin = lax.fori_loop(0, ntiles, att_body, init)
    for g in range(N_KV_HEADS):
        o = acc_ref[g] * (1.0 / fin[2 * g + 1])
        for h in range(Q_PER_KV):
            hd = g * Q_PER_KV + h
            o_ref[:, hd * HEAD_DIM:(hd + 1) * HEAD_DIM] = o[:, h * QBLK:(h + 1) * QBLK].T.astype(jnp.bfloat16)


def _dsa(qi, wit, q, ki16, k16, vt, *, ksel, kt, vis_step, vis_base, lim_lo, lim_hi):
    NB, L, _ = ki16.shape
    Sq = wit.shape[2]
    nq = Sq // QBLK
    assert L % kt == 0
    kern = functools.partial(_dsa_kernel, ksel=ksel, KT=kt, vis_step=vis_step, vis_base=vis_base,
                             lim_lo=lim_lo, lim_hi=lim_hi)
    return pl.pallas_call(
        kern,
        grid=(NB, nq),
        in_specs=[pl.BlockSpec((QBLK, qi.shape[1]), lambda b, j: (b * nq + j, 0)),
                  pl.BlockSpec((None, N_IDX_HEADS, QBLK), lambda b, j: (b, 0, j)),
                  pl.BlockSpec((QBLK, q.shape[1]), lambda b, j: (b * nq + j, 0)),
                  pl.BlockSpec((None, L, LANES), lambda b, j: (b, 0, 0)),
                  pl.BlockSpec((None, L, k16.shape[2]), lambda b, j: (b, 0, 0)),
                  pl.BlockSpec((None, vt.shape[1], L), lambda b, j: (b, 0, 0))],
        out_specs=pl.BlockSpec((QBLK, q.shape[1]), lambda b, j: (b * nq + j, 0)),
        out_shape=jax.ShapeDtypeStruct(q.shape, jnp.bfloat16),
        scratch_shapes=[pltpu.VMEM((L, QBLK), jnp.int32),
                        pltpu.VMEM((L, QBLK), jnp.float32),
                        pltpu.VMEM((L, QBLK), jnp.int16),
                        pltpu.VMEM((SUBLANES, QBLK), jnp.int32),
                        pltpu.VMEM((N_KV_HEADS, HEAD_DIM, Q_PER_KV * QBLK), jnp.float32)],
        compiler_params=_cparams(("parallel", "arbitrary")),
        name="dsa",
    )(qi, wit, q, ki16, k16, vt)


def _merge_kernel(a_ref, b_ref, sg_ref, x_ref, gt_ref, sc2_ref, sh2_ref, g2_ref,
                  wa_ref, wb_ref, wo_ref, wq_ref, xm_ref, h2_ref, qp_ref):
    D = x_ref.shape[1]
    sg = sg_ref[...].astype(jnp.float32)
    merged = sg[:, :D] * _bdot(a_ref[...], wa_ref[...]) + sg[:, D:] * _bdot(b_ref[...], wb_ref[...])
    y = _bdot(merged.astype(jnp.bfloat16), wo_ref[...])
    xm = x_ref[...] + gt_ref[...] * y
    xm_ref[...] = xm
    h2 = _modulate(xm, g2_ref[...], sc2_ref[...], sh2_ref[...])
    h2_ref[...] = h2
    qp_ref[...] = _bdot(h2.astype(jnp.bfloat16), wq_ref[...]).astype(jnp.bfloat16)


def _merge(a, b, sg, x2d, gt, sc2, sh2, g2, wa, wb, wo, wq, *, tm):
    T, D = x2d.shape
    nt = T // tm
    n_mod, R, _ = gt.shape
    tiles_per_mod = nt // n_mod
    const2 = lambda i: (0, 0)
    row = lambda w: pl.BlockSpec((tm, w), lambda i: (i, 0))
    mod_spec = pl.BlockSpec((None, R, D), lambda i: (i // tiles_per_mod, 0, 0))
    nq = wq.shape[1]
    return pl.pallas_call(
        _merge_kernel,
        grid=(nt,),
        in_specs=[row(a.shape[1]), row(b.shape[1]), row(sg.shape[1]), row(D), mod_spec, mod_spec, mod_spec,
                  pl.BlockSpec((1, D), const2)] + [pl.BlockSpec(w.shape, const2) for w in (wa, wb, wo, wq)],
        out_specs=[row(D), row(D), row(nq)],
        out_shape=[jax.ShapeDtypeStruct((T, D), jnp.float32), jax.ShapeDtypeStruct((T, D), jnp.float32),
                   jax.ShapeDtypeStruct((T, nq), jnp.bfloat16)],
        compiler_params=_cparams(("parallel",)),
        name="merge",
    )(a, b, sg, x2d, gt, sc2, sh2, g2, wa, wb, wo, wq)


_PAIR_COUNTS = [PEER_TOPK // (a + 1) for a in range(PEER_TOPK)]
_PAIR_OFFS = [int(v) for v in np.cumsum([0] + _PAIR_COUNTS[:-1])]
_N_PAIRS = sum(_PAIR_COUNTS)
_PAIR_ROWS = -(-_N_PAIRS // SUBLANES) * SUBLANES


def _route_kernel(qp_ref, sk_ref, e_ref, g_ref, s_ref, sv_ref, si_ref, cand_ref, pay_ref, gv_ref):
    TT = qp_ref.shape[0]
    half = PEER_KEY_DIM // 2
    NC = PEER_HEADS * 2
    NEG_INF = -jnp.inf

    for c in range(NC):
        s_ref[c] = _dot_nt(sk_ref[c // 2, c % 2], qp_ref[:, c * half:(c + 1) * half])

    rowk = lax.broadcasted_iota(jnp.int32, (PEER_N_KEYS, TT), 0)

    def key_round(r, carry):
        for c in range(NC):
            s = s_ref[c]
            m = jnp.max(s, axis=0, keepdims=True)
            am = jnp.min(jnp.where(s == m, rowk, PEER_N_KEYS), axis=0, keepdims=True)
            sv_ref[c, pl.ds(r, 1), :] = m
            si_ref[c, pl.ds(r, 1), :] = am
            s_ref[c] = jnp.where(rowk == am, NEG_INF, s)
        return carry

    lax.fori_loop(0, PEER_TOPK, key_round, 0)

    for h in range(PEER_HEADS):
        sv0, sv1 = sv_ref[2 * h], sv_ref[2 * h + 1]
        si0, si1 = si_ref[2 * h], si_ref[2 * h + 1]
        cand_ref[h, _PAIR_ROWS - SUBLANES:_PAIR_ROWS, :] = jnp.full((SUBLANES, TT), NEG_INF, jnp.float32)
        pay_ref[h, _PAIR_ROWS - SUBLANES:_PAIR_ROWS, :] = jnp.zeros((SUBLANES, TT), jnp.int32)
        for a in range(PEER_TOPK):
            n, off = _PAIR_COUNTS[a], _PAIR_OFFS[a]
            cand_ref[h, off:off + n, :] = sv0[a:a + 1, :] + sv1[0:n, :]
            pay_ref[h, off:off + n, :] = si0[a:a + 1, :] * PEER_N_KEYS + si1[0:n, :]

    rowp = lax.broadcasted_iota(jnp.int32, (_PAIR_ROWS, TT), 0)

    def pair_round(r, carry):
        for h in range(PEER_HEADS):
            s = cand_ref[h]
            m = jnp.max(s, axis=0, keepdims=True)
            am = jnp.min(jnp.where(s == m, rowp, _PAIR_ROWS), axis=0, keepdims=True)
            hit = rowp == am
            gv_ref[h, pl.ds(r, 1), :] = m
            e_ref[pl.ds(h * PEER_TOPK + r, 1), :] = jnp.sum(jnp.where(hit, pay_ref[h], 0), axis=0, keepdims=True)
            cand_ref[h] = jnp.where(hit, NEG_INF, s)
        return carry

    lax.fori_loop(0, PEER_TOPK, pair_round, 0)

    for h in range(PEER_HEADS):
        gv = gv_ref[h]
        ex = jnp.exp(gv - jnp.max(gv, axis=0, keepdims=True))
        g_ref[h * PEER_TOPK:(h + 1) * PEER_TOPK, :] = ex / jnp.sum(ex, axis=0, keepdims=True)


def _route(qp, sk16, *, tt):
    T = qp.shape[0]
    nt = T // tt
    ne = PEER_HEADS * PEER_TOPK
    out = jax.ShapeDtypeStruct((nt, ne, tt), jnp.int32)
    blk = pl.BlockSpec((None, ne, tt), lambda i: (i, 0, 0))
    return pl.pallas_call(
        _route_kernel,
        grid=(nt,),
        in_specs=[pl.BlockSpec((tt, qp.shape[1]), lambda i: (i, 0)),
                  pl.BlockSpec(sk16.shape, lambda i: (0, 0, 0, 0))],
        out_specs=[blk, blk],
        out_shape=[out, jax.ShapeDtypeStruct(out.shape, jnp.float32)],
        scratch_shapes=[pltpu.VMEM((PEER_HEADS * 2, PEER_N_KEYS, tt), jnp.float32),
                        pltpu.VMEM((PEER_HEADS * 2, PEER_TOPK, tt), jnp.float32),
                        pltpu.VMEM((PEER_HEADS * 2, PEER_TOPK, tt), jnp.int32),
                        pltpu.VMEM((PEER_HEADS, _PAIR_ROWS, tt), jnp.float32),
                        pltpu.VMEM((PEER_HEADS, _PAIR_ROWS, tt), jnp.int32),
                        pltpu.VMEM((PEER_HEADS, PEER_TOPK, tt), jnp.float32)],
        compiler_params=_cparams(("parallel",)),
        name="peer_route",
    )(qp, sk16)


def _peer_kernel(ids_ref, g_ref, h2_ref, xm_ref, gt_ref, w_hbm, o_ref, *scratch):
    TP, D = h2_ref.shape
    NE = g_ref.shape[1]
    HS = D // 2 // LANES
    NG = TP // PEER_GROUP
    bf = jnp.bfloat16
    wbufs, sem = scratch[:-1], scratch[-1]

    def issue(grp, half):
        for p in range(PEER_GROUP):
            base = (grp * PEER_GROUP + p) * NE
            slot = half * PEER_GROUP + p
            for k in range(NE):
                src = pl.multiple_of(ids_ref[base + k], PEER_ROWS)
                pltpu.make_async_copy(w_hbm.at[pl.ds(src, PEER_ROWS)],
                                      wbufs[slot].at[pl.ds(k * PEER_ROWS, PEER_ROWS)],
                                      sem.at[slot]).start(priority=k % 2)

    def wait(slot):
        pltpu.make_async_copy(w_hbm.at[pl.ds(0, NE * PEER_ROWS)], wbufs[slot], sem.at[slot]).wait()

    def table(slot, r0):
        lo, hi = [], []
        for s in range(HS):
            w = wbufs[slot][pl.ds(r0 + s, NE, stride=PEER_ROWS), :]
            lo.append(pltpu.bitcast(w << 16, jnp.float32).astype(bf))
            hi.append(pltpu.bitcast(w & jnp.int32(-65536), jnp.float32).astype(bf))
        return jnp.concatenate(lo + hi, axis=1)

    for g0 in range(PEER_SETS - 1):
        issue(g0, g0)

    def ring(i, carry):
        for r in range(PEER_SETS):
            group(PEER_SETS * i + r, r, True)
        return carry

    def group(grp, half, issue_ahead):
        slots = [half * PEER_GROUP + p for p in range(PEER_GROUP)]
        toks = [grp * PEER_GROUP + p for p in range(PEER_GROUP)]
        for slot in slots:
            wait(slot)
        if issue_ahead:
            issue(grp + PEER_SETS - 1, (half + PEER_SETS - 1) % PEER_SETS)
        acts = []
        for t, slot in zip(toks, slots):
            x = jnp.broadcast_to(h2_ref[pl.ds(t, 1), :], (SUBLANES, D)).astype(bf)
            acts.append(_dot_nt(x, table(slot, 0))[0:1, :])
        ys = []
        for t, slot, act in zip(toks, slots, acts):
            c = g_ref[pl.ds(t, 1), :] * jax.nn.gelu(act)
            c8 = jnp.broadcast_to(c, (SUBLANES, NE)).astype(bf)
            ys.append(_bdot(c8, table(slot, HS))[0:1, :])
        for t, y in zip(toks, ys):
            gt = gt_ref[pl.ds(t, 1), :] if gt_ref.shape[0] == TP else gt_ref[...]
            o_ref[pl.ds(t, 1), :] = xm_ref[pl.ds(t, 1), :] + gt * y

    n_ring = (NG - (PEER_SETS - 1)) // PEER_SETS
    lax.fori_loop(0, n_ring, ring, 0)
    for grp in range(n_ring * PEER_SETS, NG):
        group(grp, grp % PEER_SETS, grp + PEER_SETS - 1 < NG)


def _peer(ids, g, h2, xm, gt, wtab):
    T, D = h2.shape
    NE = ids.shape[1]
    nt = T // PEER_TOK
    n_mod, R, _ = gt.shape
    tiles_per_mod = nt // n_mod
    row = lambda w: pl.BlockSpec((PEER_TOK, w), lambda i: (i, 0))
    return pl.pallas_call(
        _peer_kernel,
        grid=(nt,),
        in_specs=[pl.BlockSpec((PEER_TOK * NE,), lambda i: (i,), memory_space=pltpu.SMEM),
                  row(NE), row(D), row(D),
                  pl.BlockSpec((None, R, D), lambda i: (i // tiles_per_mod, 0, 0)),
                  pl.BlockSpec(memory_space=pl.ANY)],
        out_specs=row(D),
        out_shape=jax.ShapeDtypeStruct((T, D), jnp.float32),
        scratch_shapes=[pltpu.VMEM((NE * PEER_ROWS, LANES), jnp.int32)] * (PEER_SETS * PEER_GROUP)
        + [pltpu.SemaphoreType.DMA((PEER_SETS * PEER_GROUP,))],
        compiler_params=_cparams(("arbitrary",)),
        name="peer_gather",
    )(ids.reshape(-1), g, h2, xm, gt, wtab)


def _pack_bf16_pairs(t):
    E, D = t.shape
    b = lax.bitcast_convert_type(t.astype(jnp.bfloat16), jnp.uint16).astype(jnp.uint32)
    w = b[:, :D // 2] | (b[:, D // 2:] << 16)
    return lax.bitcast_convert_type(w, jnp.int32).reshape(E, D // 2 // LANES, LANES)


def _rope_tables(pos, rot_dim, width):
    half = rot_dim // 2
    inv_freq = ROPE_THETA ** (-jnp.arange(half, dtype=jnp.float32) / half)
    ang = pos.astype(jnp.float32)[..., None] * inv_freq
    cos, sin = jnp.cos(ang), jnp.sin(ang)
    w = np.arange(LANES) % width
    first, second = w < half, (w >= half) & (w < rot_dim)
    src = np.where(first, w, np.where(second, w - half, 0))
    c = jnp.where(first | second, cos[:, src], 1.0)
    sa = jnp.where(first, -sin[:, src], 0.0)
    sb = jnp.where(second, sin[:, src], 0.0)
    return c, sa, sb


def _gmlp_mask():
    i = np.arange(GMLP_CHUNK)
    return (i[None, :] // CHUNK) <= (i[:, None] // CHUNK)


def _rows_per_seq(v, reps, tm):
    nb, d = v.shape
    return jnp.repeat(v, reps, axis=0).reshape(nb * reps // tm, tm, d)


def _layer(xp, xs, cache_k, cache_v, cache_kidx, c_prompt, c_sample, ada_w, ada_b, norm1_g, norm2_g, w_in,
           q_norm_g, k_norm_g, kidx_norm_g, gmlp_v_norm_g, gmlp_ws, gmlp_b, w_branch_a, w_branch_b, w_out,
           peer_wq, peer_subkeys, peer_u, peer_v):
    B, S, D = xp.shape
    DB, DS, _ = xs.shape
    P = cache_k.shape[1]
    Tp, Ts = B * S, DB * DS
    tm_p, tm_s = 2 * GMLP_CHUNK, GMLP_CHUNK
    assert S % tm_p == 0 and GMLP_CHUNK % DS == 0 and Ts % tm_s == 0 and DS <= CHUNK and P % CHUNK == 0
    assert Tp % PEER_TOK == 0 and Ts % PEER_TOK == 0 and S % PEER_TOK == 0
    bf = jnp.bfloat16
    f32 = jnp.float32

    nrow = B + DB
    npad = -nrow % SUBLANES
    c_all = jnp.concatenate([c_prompt, c_sample, jnp.zeros((npad, D), f32)], axis=0)
    mod = _adaln(c_all, ada_w, ada_b)
    mods = jnp.split(mod, 6, axis=-1)
    mp = [m[:B].reshape(B, 1, D) for m in mods]
    ms = [_rows_per_seq(m[B:B + DB], DS, tm_s) for m in mods]

    sizes = (GMLP_WIDTH, GMLP_WIDTH, N_HEADS * HEAD_DIM, N_KV_HEADS * HEAD_DIM, N_KV_HEADS * HEAD_DIM,
             N_IDX_HEADS * IDX_DIM, IDX_DIM, N_IDX_HEADS, 2 * D)
    pts = [int(s) for s in np.cumsum(sizes)[:-1]]
    wu, wv, wq, wk, wva, wqi, wki, wwi, wg = jnp.split(w_in.astype(bf), pts, axis=-1)
    wkw = jnp.concatenate([wki, wwi, jnp.zeros((D, LANES - IDX_DIM - N_IDX_HEADS), bf)], axis=-1)
    wts = (wu, wv, wq, wk, wva, wqi, wg, wkw)
    kig = jnp.concatenate([kidx_norm_g, jnp.zeros((LANES - IDX_DIM,), f32)]).reshape(1, LANES)
    gains = (q_norm_g.reshape(1, HEAD_DIM), k_norm_g.reshape(1, HEAD_DIM), kig, gmlp_v_norm_g.reshape(1, GMLP_WIDTH))
    g1 = norm1_g.reshape(1, D)
    g2 = norm2_g.reshape(1, D)

    wmask = jnp.where(_gmlp_mask()[None], gmlp_ws, 0.0)
    wm_p = wmask.astype(bf)
    gb_p = jnp.repeat(gmlp_b.T, GMLP_GROUP_DIM, axis=1)
    reps = GMLP_CHUNK // DS
    eye = jnp.eye(reps, dtype=f32)
    wm_s = jnp.einsum("ab,gij->gaibj", eye, wmask[:, :DS, :DS]).reshape(GMLP_GROUPS, GMLP_CHUNK, GMLP_CHUNK).astype(bf)
    gb_s = jnp.tile(gb_p[:DS], (reps, 1))

    pos_p = jnp.arange(S)
    pos_s = P + (jnp.arange(tm_s) % DS)
    tabs_p = _rope_tables(pos_p, ROT_DIM, HEAD_DIM) + _rope_tables(pos_p, IDX_ROT_DIM, IDX_DIM)
    tabs_s = _rope_tables(pos_s, ROT_DIM, HEAD_DIM) + _rope_tables(pos_s, IDX_ROT_DIM, IDX_DIM)

    x2p = xp.reshape(Tp, D)
    x2s = xs.reshape(Ts, D)

    (a_p, q_p, k32_p, k16_p, va32_p, va16_p, qi_p, kw32_p, ki16_p, sg_p) = _inproj(
        x2p, mp[1], mp[0], g1, wts, tabs_p, gains, wm_p, gb_p, tm=tm_p, tab_blocks=S // tm_p, emit_v=False)
    wit_p = kw32_p[:, IDX_DIM:IDX_DIM + N_IDX_HEADS].reshape(B, S, N_IDX_HEADS).transpose(0, 2, 1)
    vt_p = va16_p.reshape(B, S, -1).transpose(0, 2, 1)
    b_p = _dsa(qi_p, wit_p, q_p, ki16_p.reshape(B, S, LANES), k16_p.reshape(B, S, -1), vt_p,
               ksel=min(TOPK_MAX, S // 4), kt=min(KT_PROMPT, S), vis_step=QBLK, vis_base=QBLK,
               lim_lo=CHUNK, lim_hi=2 * CHUNK)

    (a_s, v_s, q_s, k32_s, k16_s, va32_s, va16_s, qi_s, kw32_s, ki16_s, sg_s) = _inproj(
        x2s, ms[1], ms[0], g1, wts, tabs_s, gains, wm_s, gb_s, tm=tm_s, tab_blocks=1, emit_v=True)
    Lk = P + DS
    Lpad = -(-Lk // KT_SAMPLE) * KT_SAMPLE

    def pad_q(a):
        w = a.shape[-1]
        return jnp.pad(a.reshape(DB, DS, w), ((0, 0), (0, QBLK - DS), (0, 0))).reshape(DB * QBLK, w)

    def cat_keys(cache, new, w):
        return jnp.concatenate([cache.reshape(DB, P, -1).astype(bf), new.reshape(DB, DS, -1),
                                jnp.zeros((DB, Lpad - Lk, w), bf)], axis=1)

    wit_s = jnp.pad(kw32_s[:, IDX_DIM:IDX_DIM + N_IDX_HEADS].reshape(DB, DS, N_IDX_HEADS).transpose(0, 2, 1),
                    ((0, 0), (0, 0), (0, QBLK - DS)))
    kidx_cache = jnp.pad(cache_kidx, ((0, 0), (0, 0), (0, LANES - IDX_DIM)))
    ki_all = cat_keys(kidx_cache, ki16_s, LANES)
    k_all = cat_keys(cache_k, k16_s, N_KV_HEADS * HEAD_DIM)
    vt_all = cat_keys(cache_v, va16_s, N_KV_HEADS * HEAD_DIM).transpose(0, 2, 1)
    b_s = _dsa(pad_q(qi_s), wit_s, pad_q(q_s), ki_all, k_all, vt_all,
               ksel=min(TOPK_MAX, Lk // 4), kt=KT_SAMPLE, vis_step=0, vis_base=Lpad, lim_lo=Lk, lim_hi=Lk)
    b_s = b_s.reshape(DB, QBLK, -1)[:, :DS].reshape(Ts, -1)

    wa, wb, wo, wpq = w_branch_a.astype(bf), w_branch_b.astype(bf), w_out.astype(bf), peer_wq.astype(bf)
    sk16 = peer_subkeys.astype(bf)
    wtab = jnp.concatenate([_pack_bf16_pairs(peer_u), _pack_bf16_pairs(peer_v)], axis=1).reshape(-1, LANES)

    def tail(a, b, sg, x2d, m, tm):
        xm, h2, qp = _merge(a, b, sg, x2d, m[2], m[4], m[3], g2, wa, wb, wo, wpq, tm=tm)
        e, g = _route(qp, sk16, tt=LANES)
        T = x2d.shape[0]
        ids = e.transpose(0, 2, 1).reshape(T, PEER_HEADS * PEER_TOPK) * PEER_ROWS
        gw = g.transpose(0, 2, 1).reshape(T, PEER_HEADS * PEER_TOPK)
        return ids, gw, h2, xm

    ids_p, gw_p, h2_p, xm_p = tail(a_p, b_p, sg_p, x2p, mp, tm_p)
    out_p = _peer(ids_p, gw_p, h2_p, xm_p, mp[5], wtab)
    ms_peer = [_rows_per_seq(m[B:B + DB], DS, PEER_TOK) for m in mods]
    ids_s, gw_s, h2_s, xm_s = tail(a_s, b_s, sg_s, x2s, ms, tm_s)
    out_s = _peer(ids_s, gw_s, h2_s, xm_s, ms_peer[5], wtab)

    new = (k32_p.reshape(B, S, N_KV_HEADS, HEAD_DIM), va32_p.reshape(B, S, N_KV_HEADS, HEAD_DIM),
           kw32_p[:, :IDX_DIM].reshape(B, S, IDX_DIM),
           k32_s.reshape(DB, DS, N_KV_HEADS, HEAD_DIM), va32_s.reshape(DB, DS, N_KV_HEADS, HEAD_DIM),
           kw32_s[:, :IDX_DIM].reshape(DB, DS, IDX_DIM), v_s.reshape(DB, DS, GMLP_WIDTH))
    return out_p.reshape(B, S, D), out_s.reshape(DB, DS, D), new


def kernel(x_prompt, x_sample, cache_k, cache_v, cache_kidx, c_prompt, c_sample, ada_w, ada_b, norm1_g, norm2_g,
           w_in, q_norm_g, k_norm_g, kidx_norm_g, gmlp_v_norm_g, gmlp_ws, gmlp_b, w_branch_a, w_branch_b, w_out,
           peer_wq, peer_subkeys, peer_u, peer_v):
    xp, xs = x_prompt, x_sample
    per_layer = []
    for l in range(ada_w.shape[0]):
        xp, xs, new = _layer(xp, xs, cache_k[l], cache_v[l], cache_kidx[l], c_prompt, c_sample, ada_w[l], ada_b[l],
                             norm1_g[l], norm2_g[l], w_in[l], q_norm_g[l], k_norm_g[l], kidx_norm_g[l],
                             gmlp_v_norm_g[l], gmlp_ws[l], gmlp_b[l], w_branch_a[l], w_branch_b[l], w_out[l],
                             peer_wq[l], peer_subkeys[l], peer_u[l], peer_v[l])
        per_layer.append(new)
    stacked = tuple(jnp.stack([n[i] for n in per_layer]) for i in range(7))
    return (xp, xs) + stacked
```

```python
import functools

import jax
import jax.numpy as jnp
import numpy as np
from jax import lax
from jax.experimental import pallas as pl
from jax.experimental.pallas import tpu as pltpu

CHUNK = 64
EPS = 1e-6
ROPE_THETA = 500000.0
GMLP_CHUNK = 128
GMLP_GROUPS = 8
GMLP_GROUP_DIM = 128
GMLP_WIDTH = GMLP_GROUPS * GMLP_GROUP_DIM
N_HEADS = 8
N_KV_HEADS = 2
Q_PER_KV = N_HEADS // N_KV_HEADS
HEAD_DIM = 128
ROT_DIM = HEAD_DIM // 4
N_IDX_HEADS = 8
IDX_DIM = 64
IDX_ROT_DIM = IDX_DIM // 4
TOPK_MAX = 256
PEER_HEADS = 8
PEER_N_KEYS = 128
PEER_KEY_DIM = 256
PEER_TOPK = 16

LANES = 128
SUBLANES = 8
VMEM_LIMIT_BYTES = 56 * 1024 * 1024

QBLK = LANES
KT_PROMPT = 1024
KT_SAMPLE = 256
KT_COUNT = 512
QK_FOLD = HEAD_DIM ** -0.5 * float(np.log2(np.e))
PEER_TOK = 256
PEER_SETS = 3
PEER_GROUP = 4
PEER_ROWS = 8
INT_MIN = -(2 ** 31)
NEG_BIG = -1e30
IDX_BIG = 2 ** 30


def _cparams(sem):
    return pltpu.CompilerParams(dimension_semantics=sem, vmem_limit_bytes=VMEM_LIMIT_BYTES)


def _rms(x, width):
    return x * lax.rsqrt(jnp.sum(x * x, axis=-1, keepdims=True) * (1.0 / width) + EPS)


def _modulate(x, g, sc, sh):
    return _rms(x, x.shape[-1]) * g * (1.0 + sc) + sh


def _rope(x, c, sa, sb, half):
    n = x.shape[-1]
    return x * c + pltpu.roll(x, n - half, 1) * sa + pltpu.roll(x, half, 1) * sb


def _bdot(a, b):
    return jnp.dot(a, b, preferred_element_type=jnp.float32)


def _dot_nt(a, b):
    return lax.dot_general(a, b, (((1,), (1,)), ((), ())), preferred_element_type=jnp.float32)


def _adaln_kernel(c_ref, w_ref, b_ref, o_ref):
    c = c_ref[...]
    h = (c * jax.nn.sigmoid(c)).astype(jnp.bfloat16)
    o_ref[...] = _bdot(h, w_ref[...].astype(jnp.bfloat16)) + b_ref[...]


def _adaln(c, w, b):
    rows, d = c.shape
    n = w.shape[1]
    tn = n // 4
    return pl.pallas_call(
        _adaln_kernel,
        grid=(n // tn,),
        in_specs=[pl.BlockSpec((rows, d), lambda j: (0, 0)),
                  pl.BlockSpec((d, tn), lambda j: (0, j)),
                  pl.BlockSpec((1, tn), lambda j: (0, j))],
        out_specs=pl.BlockSpec((rows, tn), lambda j: (0, j)),
        out_shape=jax.ShapeDtypeStruct((rows, n), jnp.float32),
        compiler_params=_cparams(("arbitrary",)),
        name="adaln",
    )(c, w, b.reshape(1, n))


def _inproj_kernel(x_ref, sc_ref, sh_ref, g1_ref,
                   wu_ref, wv_ref, wq_ref, wk_ref, wva_ref, wqi_ref, wg_ref, wkw_ref,
                   c128_ref, sa128_ref, sb128_ref, c64_ref, sa64_ref, sb64_ref,
                   qg_ref, kg_ref, kig_ref, gvg_ref, wm_ref, gb_ref,
                   *out_refs, emit_v):
    if emit_v:
        (a_ref, v_ref, q_ref, k32_ref, k16_ref, va32_ref, va16_ref, qi_ref, kw32_ref, ki16_ref, sg_ref) = out_refs
    else:
        (a_ref, q_ref, k32_ref, k16_ref, va32_ref, va16_ref, qi_ref, kw32_ref, ki16_ref, sg_ref) = out_refs
        v_ref = None
    tm = x_ref.shape[0]
    h = _modulate(x_ref[...], g1_ref[...], sc_ref[...], sh_ref[...]).astype(jnp.bfloat16)

    c128, sa128, sb128 = c128_ref[...], sa128_ref[...], sb128_ref[...]
    c64, sa64, sb64 = c64_ref[...], sa64_ref[...], sb64_ref[...]

    v = _rms(jax.nn.gelu(_bdot(h, wv_ref[...])), GMLP_WIDTH) * gvg_ref[...]
    if v_ref is not None:
        v_ref[...] = v
    v16 = v.astype(jnp.bfloat16)
    u = jax.nn.gelu(_bdot(h, wu_ref[...]))
    for c in range(tm // GMLP_CHUNK):
        rows = slice(c * GMLP_CHUNK, (c + 1) * GMLP_CHUNK)
        for g in range(GMLP_GROUPS):
            cols = slice(g * GMLP_GROUP_DIM, (g + 1) * GMLP_GROUP_DIM)
            s = _bdot(wm_ref[g], v16[rows, cols]) + gb_ref[:, cols]
            a_ref[rows, cols] = (u[rows, cols] * s).astype(jnp.bfloat16)

    q = _bdot(h, wq_ref[...])
    for hd in range(N_HEADS):
        cols = slice(hd * HEAD_DIM, (hd + 1) * HEAD_DIM)
        qh = _rope(_rms(q[:, cols], HEAD_DIM) * qg_ref[...], c128, sa128, sb128, ROT_DIM // 2)
        q_ref[:, cols] = (qh * QK_FOLD).astype(jnp.bfloat16)
    k = _bdot(h, wk_ref[...])
    for hd in range(N_KV_HEADS):
        cols = slice(hd * HEAD_DIM, (hd + 1) * HEAD_DIM)
        kh = _rope(_rms(k[:, cols], HEAD_DIM) * kg_ref[...], c128, sa128, sb128, ROT_DIM // 2)
        k32_ref[:, cols] = kh
        k16_ref[:, cols] = kh.astype(jnp.bfloat16)
    va = _bdot(h, wva_ref[...])
    va32_ref[...] = va
    va16_ref[...] = va.astype(jnp.bfloat16)

    qi = _bdot(h, wqi_ref[...])
    for p in range(N_IDX_HEADS * IDX_DIM // LANES):
        cols = slice(p * LANES, (p + 1) * LANES)
        qi_ref[:, cols] = _rope(qi[:, cols], c64, sa64, sb64, IDX_ROT_DIM // 2).astype(jnp.bfloat16)
    kw = _bdot(h, wkw_ref[...])
    is_ki = lax.broadcasted_iota(jnp.int32, kw.shape, 1) < IDX_DIM
    kis = jnp.where(is_ki, kw, 0.0)
    kin = kis * lax.rsqrt(jnp.sum(kis * kis, axis=-1, keepdims=True) * (1.0 / IDX_DIM) + EPS) * kig_ref[...]
    kir = _rope(kin, c64, sa64, sb64, IDX_ROT_DIM // 2)
    kw32_ref[...] = jnp.where(is_ki, kir, kw * (N_IDX_HEADS ** -0.5))
    ki16_ref[...] = jnp.where(is_ki, kir, 0.0).astype(jnp.bfloat16)

    sg_ref[...] = jax.nn.sigmoid(_bdot(h, wg_ref[...])).astype(jnp.bfloat16)


def _inproj(x2d, sc, sh, g1, wts, tabs, gains, wm, gb, *, tm, tab_blocks, emit_v):
    T, D = x2d.shape
    nt = T // tm
    n_mod, R, _ = sc.shape
    tiles_per_mod = nt // n_mod
    const2 = lambda i: (0, 0)
    mod_spec = pl.BlockSpec((None, R, D), lambda i: (i // tiles_per_mod, 0, 0))
    tab_spec = pl.BlockSpec((tm, LANES), lambda i: (i % tab_blocks, 0))
    in_specs = ([pl.BlockSpec((tm, D), lambda i: (i, 0)), mod_spec, mod_spec, pl.BlockSpec((1, D), const2)]
                + [pl.BlockSpec(w.shape, const2) for w in wts]
                + [tab_spec] * 6
                + [pl.BlockSpec(g.shape, const2) for g in gains]
                + [pl.BlockSpec(wm.shape, lambda i: (0, 0, 0)), pl.BlockSpec(gb.shape, const2)])
    widths = [(GMLP_WIDTH, jnp.bfloat16)]
    if emit_v:
        widths.append((GMLP_WIDTH, jnp.float32))
    widths += [(N_HEADS * HEAD_DIM, jnp.bfloat16),
               (N_KV_HEADS * HEAD_DIM, jnp.float32), (N_KV_HEADS * HEAD_DIM, jnp.bfloat16),
               (N_KV_HEADS * HEAD_DIM, jnp.float32), (N_KV_HEADS * HEAD_DIM, jnp.bfloat16),
               (N_IDX_HEADS * IDX_DIM, jnp.bfloat16),
               (LANES, jnp.float32), (LANES, jnp.bfloat16),
               (2 * D, jnp.bfloat16)]
    return pl.pallas_call(
        functools.partial(_inproj_kernel, emit_v=emit_v),
        grid=(nt,),
        in_specs=in_specs,
        out_specs=[pl.BlockSpec((tm, w), lambda i: (i, 0)) for w, _ in widths],
        out_shape=[jax.ShapeDtypeStruct((T, w), dt) for w, dt in widths],
        compiler_params=_cparams(("parallel",)),
        name="inproj_v" if emit_v else "inproj",
    )(x2d, sc, sh, g1, *wts, *tabs, *gains, wm, gb)


def _dsa_kernel(qi_ref, wi_ref, q_ref, ki_ref, k_ref, vt_ref, o_ref,
                keys_ref, bias_ref, meff_ref, acc_ref,
                *, ksel, KT, vis_step, vis_base, lim_lo, lim_hi):
    j = pl.program_id(1)
    L = ki_ref.shape[0]
    ntiles = jnp.minimum((j * vis_step + vis_base + KT - 1) // KT, L // KT)
    lane = lax.broadcasted_iota(jnp.int32, (1, QBLK), 1)
    limit = j * vis_step + jnp.where(lane < CHUNK, lim_lo, lim_hi)
    row = lax.broadcasted_iota(jnp.int32, (KT, QBLK), 0)

    def tile_off(t):
        return pl.multiple_of(t * KT, KT)

    wi = wi_ref[...] * (IDX_DIM ** -0.5)
    qit = qi_ref[...].astype(jnp.float32).T.astype(jnp.bfloat16)
    qit = jnp.concatenate([qit[h * IDX_DIM:(h + 1) * IDX_DIM, :] for h in range(N_IDX_HEADS)], axis=1)
    qit = jnp.concatenate([qit, jnp.zeros((LANES - IDX_DIM, N_IDX_HEADS * QBLK), jnp.bfloat16)], axis=0)

    def idx_body(t, carry):
        off = tile_off(t)
        s = _bdot(ki_ref[pl.ds(off, KT), :], qit)
        acc = jnp.zeros((KT, QBLK), jnp.float32)
        for h in range(N_IDX_HEADS):
            acc = acc + jnp.maximum(s[:, h * QBLK:(h + 1) * QBLK], 0.0) * wi[h:h + 1, :]
        bits = pltpu.bitcast(acc, jnp.int32)
        key = jnp.where(bits >= 0, bits, bits ^ 0x7FFFFFFF)
        keys_ref[pl.ds(off, KT), :] = jnp.where(row + off < limit, key, INT_MIN)
        return carry

    lax.fori_loop(0, ntiles, idx_body, 0)

    KC = min(KT, KT_COUNT)
    nctiles = jnp.minimum((j * vis_step + vis_base + KC - 1) // KC, L // KC)
    rowc = lax.broadcasted_iota(jnp.int32, (KC, QBLK), 0)

    def count(indicator):
        def body(t, acc):
            off = pl.multiple_of(t * KC, KC)
            ones = indicator(keys_ref[pl.ds(off, KC), :], off)
            return acc + ones.reshape(KC // SUBLANES, SUBLANES, QBLK).sum(axis=0)
        acc = lax.fori_loop(0, nctiles, body, jnp.zeros((SUBLANES, QBLK), jnp.int32))
        return acc.sum(axis=0, keepdims=True)

    def bit_body(i, tu):
        cand_u = tu | lax.shift_left(jnp.int32(1), 31 - i)
        cand_s = cand_u ^ INT_MIN
        cnt = count(lambda key, off: jnp.where(key >= cand_s, 1, 0))
        return jnp.where(cnt >= ksel, cand_u, tu)

    tu = lax.fori_loop(0, 32, bit_body, jnp.zeros((1, QBLK), jnp.int32))
    thr = jnp.maximum(tu ^ INT_MIN, INT_MIN + 1)

    need = count(lambda key, off: jnp.where(key >= thr, 1, 0)) > ksel
    meff_ref[...] = jnp.full(meff_ref.shape, IDX_BIG, jnp.int32)

    @pl.when(jnp.max(need.astype(jnp.int32)) > 0)
    def _():
        want = ksel - count(lambda key, off: jnp.where(key > thr, 1, 0))
        nbits = int(L).bit_length()

        def m_body(i, mp):
            cand = mp | lax.shift_left(jnp.int32(1), nbits - 1 - i)
            f = count(lambda key, off: jnp.where(key == thr, jnp.where(rowc + off < cand, 1, 0), 0))
            return jnp.where(f < want, cand, mp)

        mp = lax.fori_loop(0, nbits, m_body, jnp.zeros((1, QBLK), jnp.int32))
        meff_ref[...] = jnp.broadcast_to(jnp.where(need, mp, IDX_BIG), meff_ref.shape)

    meff = meff_ref[0:1, :]

    def bias_body(t, carry):
        off = tile_off(t)
        key = keys_ref[pl.ds(off, KT), :]
        tie = jnp.where(row + off <= meff, 0.0, NEG_BIG)
        bias_ref[pl.ds(off, KT), :] = jnp.where(key == thr, tie, jnp.where(key > thr, 0.0, NEG_BIG))
        return carry

    lax.fori_loop(0, ntiles, bias_body, 0)

    qgs = [jnp.concatenate([q_ref[:, (g * Q_PER_KV + h) * HEAD_DIM:(g * Q_PER_KV + h + 1) * HEAD_DIM]
                            for h in range(Q_PER_KV)], axis=0) for g in range(N_KV_HEADS)]
    acc_ref[...] = jnp.zeros(acc_ref.shape, jnp.float32)
    W = Q_PER_KV * QBLK

    def att_body(t, carry):
        off = tile_off(t)
        b = bias_ref[pl.ds(off, KT), :]
        b4 = jnp.concatenate([b] * Q_PER_KV, axis=1)
        out = []
        for g in range(N_KV_HEADS):
            m_old, l_old = carry[2 * g], carry[2 * g + 1]
            kt = k_ref[pl.ds(off, KT), g * HEAD_DIM:(g + 1) * HEAD_DIM]
            s = _dot_nt(kt, qgs[g]) + b4
            m_new = jnp.maximum(m_old, jnp.max(s, axis=0, keepdims=True))
            alpha = jnp.exp2(m_old - m_new)
            p = jnp.exp2(s - m_new)
            l_new = alpha * l_old + jnp.sum(p, axis=0, keepdims=True)
            vt = vt_ref[g * HEAD_DIM:(g + 1) * HEAD_DIM, pl.ds(off, KT)]
            acc_ref[g] = acc_ref[g] * alpha + _bdot(vt, p.astype(jnp.bfloat16))
            out += [m_new, l_new]
        return tuple(out)

    init = (jnp.full((1, W), NEG_BIG, jnp.float32), jnp.zeros((1, W), jnp.float32)) * N_KV_HEADS
    fin = lax.fori_loop(0, ntiles, att_body, init)
    for g in range(N_KV_HEADS):
        o = acc_ref[g] * (1.0 / fin[2 * g + 1])
        for h in range(Q_PER_KV):
            hd = g * Q_PER_KV + h
            o_ref[:, hd * HEAD_DIM:(hd + 1) * HEAD_DIM] = o[:, h * QBLK:(h + 1) * QBLK].T.astype(jnp.bfloat16)


def _dsa(qi, wit, q, ki16, k16, vt, *, ksel, kt, vis_step, vis_base, lim_lo, lim_hi):
    NB, L, _ = ki16.shape
    Sq = wit.shape[2]
    nq = Sq // QBLK
    assert L % kt == 0
    kern = functools.partial(_dsa_kernel, ksel=ksel, KT=kt, vis_step=vis_step, vis_base=vis_base,
                             lim_lo=lim_lo, lim_hi=lim_hi)
    return pl.pallas_call(
        kern,
        grid=(NB, nq),
        in_specs=[pl.BlockSpec((QBLK, qi.shape[1]), lambda b, j: (b * nq + j, 0)),
                  pl.BlockSpec((None, N_IDX_HEADS, QBLK), lambda b, j: (b, 0, j)),
                  pl.BlockSpec((QBLK, q.shape[1]), lambda b, j: (b * nq + j, 0)),
                  pl.BlockSpec((None, L, LANES), lambda b, j: (b, 0, 0)),
                  pl.BlockSpec((None, L, k16.shape[2]), lambda b, j: (b, 0, 0)),
                  pl.BlockSpec((None, vt.shape[1], L), lambda b, j: (b, 0, 0))],
        out_specs=pl.BlockSpec((QBLK, q.shape[1]), lambda b, j: (b * nq + j, 0)),
        out_shape=jax.ShapeDtypeStruct(q.shape, jnp.bfloat16),
        scratch_shapes=[pltpu.VMEM((L, QBLK), jnp.int32),
                        pltpu.VMEM((L, QBLK), jnp.float32),
                        pltpu.VMEM((SUBLANES, QBLK), jnp.int32),
                        pltpu.VMEM((N_KV_HEADS, HEAD_DIM, Q_PER_KV * QBLK), jnp.float32)],
        compiler_params=_cparams(("parallel", "arbitrary")),
        name="dsa",
    )(qi, wit, q, ki16, k16, vt)


def _merge_kernel(a_ref, b_ref, sg_ref, x_ref, gt_ref, sc2_ref, sh2_ref, g2_ref,
                  wa_ref, wb_ref, wo_ref, wq_ref, xm_ref, h2_ref, qp_ref):
    D = x_ref.shape[1]
    sg = sg_ref[...].astype(jnp.float32)
    merged = sg[:, :D] * _bdot(a_ref[...], wa_ref[...]) + sg[:, D:] * _bdot(b_ref[...], wb_ref[...])
    y = _bdot(merged.astype(jnp.bfloat16), wo_ref[...])
    xm = x_ref[...] + gt_ref[...] * y
    xm_ref[...] = xm
    h2 = _modulate(xm, g2_ref[...], sc2_ref[...], sh2_ref[...])
    h2_ref[...] = h2
    qp_ref[...] = _bdot(h2.astype(jnp.bfloat16), wq_ref[...]).astype(jnp.bfloat16)


def _merge(a, b, sg, x2d, gt, sc2, sh2, g2, wa, wb, wo, wq, *, tm):
    T, D = x2d.shape
    nt = T // tm
    n_mod, R, _ = gt.shape
    tiles_per_mod = nt // n_mod
    const2 = lambda i: (0, 0)
    row = lambda w: pl.BlockSpec((tm, w), lambda i: (i, 0))
    mod_spec = pl.BlockSpec((None, R, D), lambda i: (i // tiles_per_mod, 0, 0))
    nq = wq.shape[1]
    return pl.pallas_call(
        _merge_kernel,
        grid=(nt,),
        in_specs=[row(a.shape[1]), row(b.shape[1]), row(sg.shape[1]), row(D), mod_spec, mod_spec, mod_spec,
                  pl.BlockSpec((1, D), const2)] + [pl.BlockSpec(w.shape, const2) for w in (wa, wb, wo, wq)],
        out_specs=[row(D), row(D), row(nq)],
        out_shape=[jax.ShapeDtypeStruct((T, D), jnp.float32), jax.ShapeDtypeStruct((T, D), jnp.float32),
                   jax.ShapeDtypeStruct((T, nq), jnp.bfloat16)],
        compiler_params=_cparams(("parallel",)),
        name="merge",
    )(a, b, sg, x2d, gt, sc2, sh2, g2, wa, wb, wo, wq)


_PAIR_COUNTS = [PEER_TOPK // (a + 1) for a in range(PEER_TOPK)]
_PAIR_OFFS = [int(v) for v in np.cumsum([0] + _PAIR_COUNTS[:-1])]
_N_PAIRS = sum(_PAIR_COUNTS)
_PAIR_ROWS = -(-_N_PAIRS // SUBLANES) * SUBLANES


def _route_kernel(qp_ref, sk_ref, e_ref, g_ref, s_ref, sv_ref, si_ref, cand_ref, pay_ref, gv_ref):
    TT = qp_ref.shape[0]
    half = PEER_KEY_DIM // 2
    NC = PEER_HEADS * 2
    NEG_INF = -jnp.inf

    for c in range(NC):
        s_ref[c] = _dot_nt(sk_ref[c // 2, c % 2], qp_ref[:, c * half:(c + 1) * half])

    rowk = lax.broadcasted_iota(jnp.int32, (PEER_N_KEYS, TT), 0)

    def key_round(r, carry):
        for c in range(NC):
            s = s_ref[c]
            m = jnp.max(s, axis=0, keepdims=True)
            am = jnp.min(jnp.where(s == m, rowk, PEER_N_KEYS), axis=0, keepdims=True)
            sv_ref[c, pl.ds(r, 1), :] = m
            si_ref[c, pl.ds(r, 1), :] = am
            s_ref[c] = jnp.where(rowk == am, NEG_INF, s)
        return carry

    lax.fori_loop(0, PEER_TOPK, key_round, 0)

    for h in range(PEER_HEADS):
        sv0, sv1 = sv_ref[2 * h], sv_ref[2 * h + 1]
        si0, si1 = si_ref[2 * h], si_ref[2 * h + 1]
        cand_ref[h, _PAIR_ROWS - SUBLANES:_PAIR_ROWS, :] = jnp.full((SUBLANES, TT), NEG_INF, jnp.float32)
        pay_ref[h, _PAIR_ROWS - SUBLANES:_PAIR_ROWS, :] = jnp.zeros((SUBLANES, TT), jnp.int32)
        for a in range(PEER_TOPK):
            n, off = _PAIR_COUNTS[a], _PAIR_OFFS[a]
            cand_ref[h, off:off + n, :] = sv0[a:a + 1, :] + sv1[0:n, :]
            pay_ref[h, off:off + n, :] = si0[a:a + 1, :] * PEER_N_KEYS + si1[0:n, :]

    rowp = lax.broadcasted_iota(jnp.int32, (_PAIR_ROWS, TT), 0)

    def pair_round(r, carry):
        for h in range(PEER_HEADS):
            s = cand_ref[h]
            m = jnp.max(s, axis=0, keepdims=True)
            am = jnp.min(jnp.where(s == m, rowp, _PAIR_ROWS), axis=0, keepdims=True)
            hit = rowp == am
            gv_ref[h, pl.ds(r, 1), :] = m
            e_ref[pl.ds(h * PEER_TOPK + r, 1), :] = jnp.sum(jnp.where(hit, pay_ref[h], 0), axis=0, keepdims=True)
            cand_ref[h] = jnp.where(hit, NEG_INF, s)
        return carry

    lax.fori_loop(0, PEER_TOPK, pair_round, 0)

    for h in range(PEER_HEADS):
        gv = gv_ref[h]
        ex = jnp.exp(gv - jnp.max(gv, axis=0, keepdims=True))
        g_ref[h * PEER_TOPK:(h + 1) * PEER_TOPK, :] = ex / jnp.sum(ex, axis=0, keepdims=True)


def _route(qp, sk16, *, tt):
    T = qp.shape[0]
    nt = T // tt
    ne = PEER_HEADS * PEER_TOPK
    out = jax.ShapeDtypeStruct((nt, ne, tt), jnp.int32)
    blk = pl.BlockSpec((None, ne, tt), lambda i: (i, 0, 0))
    return pl.pallas_call(
        _route_kernel,
        grid=(nt,),
        in_specs=[pl.BlockSpec((tt, qp.shape[1]), lambda i: (i, 0)),
                  pl.BlockSpec(sk16.shape, lambda i: (0, 0, 0, 0))],
        out_specs=[blk, blk],
        out_shape=[out, jax.ShapeDtypeStruct(out.shape, jnp.float32)],
        scratch_shapes=[pltpu.VMEM((PEER_HEADS * 2, PEER_N_KEYS, tt), jnp.float32),
                        pltpu.VMEM((PEER_HEADS * 2, PEER_TOPK, tt), jnp.float32),
                        pltpu.VMEM((PEER_HEADS * 2, PEER_TOPK, tt), jnp.int32),
                        pltpu.VMEM((PEER_HEADS, _PAIR_ROWS, tt), jnp.float32),
                        pltpu.VMEM((PEER_HEADS, _PAIR_ROWS, tt), jnp.int32),
                        pltpu.VMEM((PEER_HEADS, PEER_TOPK, tt), jnp.float32)],
        compiler_params=_cparams(("parallel",)),
        name="peer_route",
    )(qp, sk16)


def _peer_kernel(ids_ref, g_ref, h2_ref, xm_ref, gt_ref, w_hbm, o_ref, *scratch):
    TP, D = h2_ref.shape
    NE = g_ref.shape[1]
    HS = D // 2 // LANES
    NG = TP // PEER_GROUP
    bf = jnp.bfloat16
    wbufs, sem = scratch[:-1], scratch[-1]

    def issue(grp, half):
        for p in range(PEER_GROUP):
            base = (grp * PEER_GROUP + p) * NE
            slot = half * PEER_GROUP + p
            for k in range(NE):
                src = pl.multiple_of(ids_ref[base + k], PEER_ROWS)
                pltpu.make_async_copy(w_hbm.at[pl.ds(src, PEER_ROWS)],
                                      wbufs[slot].at[pl.ds(k * PEER_ROWS, PEER_ROWS)],
                                      sem.at[slot]).start(priority=k % 2)

    def wait(slot):
        pltpu.make_async_copy(w_hbm.at[pl.ds(0, NE * PEER_ROWS)], wbufs[slot], sem.at[slot]).wait()

    def table(slot, r0):
        lo, hi = [], []
        for s in range(HS):
            w = wbufs[slot][pl.ds(r0 + s, NE, stride=PEER_ROWS), :]
            lo.append(pltpu.bitcast(w << 16, jnp.float32).astype(bf))
            hi.append(pltpu.bitcast(w & jnp.int32(-65536), jnp.float32).astype(bf))
        return jnp.concatenate(lo + hi, axis=1)

    for g0 in range(PEER_SETS - 1):
        issue(g0, g0)

    def ring(i, carry):
        for r in range(PEER_SETS):
            group(PEER_SETS * i + r, r, True)
        return carry

    def group(grp, half, issue_ahead):
        slots = [half * PEER_GROUP + p for p in range(PEER_GROUP)]
        toks = [grp * PEER_GROUP + p for p in range(PEER_GROUP)]
        for slot in slots:
            wait(slot)
        if issue_ahead:
            issue(grp + PEER_SETS - 1, (half + PEER_SETS - 1) % PEER_SETS)
        acts = []
        for t, slot in zip(toks, slots):
            x = jnp.broadcast_to(h2_ref[pl.ds(t, 1), :], (SUBLANES, D)).astype(bf)
            acts.append(_dot_nt(x, table(slot, 0))[0:1, :])
        ys = []
        for t, slot, act in zip(toks, slots, acts):
            c = g_ref[pl.ds(t, 1), :] * jax.nn.gelu(act)
            c8 = jnp.broadcast_to(c, (SUBLANES, NE)).astype(bf)
            ys.append(_bdot(c8, table(slot, HS))[0:1, :])
        for t, y in zip(toks, ys):
            gt = gt_ref[pl.ds(t, 1), :] if gt_ref.shape[0] == TP else gt_ref[...]
            o_ref[pl.ds(t, 1), :] = xm_ref[pl.ds(t, 1), :] + gt * y

    n_ring = (NG - (PEER_SETS - 1)) // PEER_SETS
    lax.fori_loop(0, n_ring, ring, 0)
    for grp in range(n_ring * PEER_SETS, NG):
        group(grp, grp % PEER_SETS, grp + PEER_SETS - 1 < NG)


def _peer(ids, g, h2, xm, gt, wtab, *, tp):
    T, D = h2.shape
    NE = ids.shape[1]
    nt = T // tp
    n_mod, R, _ = gt.shape
    tiles_per_mod = nt // n_mod
    row = lambda w: pl.BlockSpec((tp, w), lambda i: (i, 0))
    return pl.pallas_call(
        _peer_kernel,
        grid=(nt,),
        in_specs=[pl.BlockSpec((tp * NE,), lambda i: (i,), memory_space=pltpu.SMEM),
                  row(NE), row(D), row(D),
                  pl.BlockSpec((None, R, D), lambda i: (i // tiles_per_mod, 0, 0)),
                  pl.BlockSpec(memory_space=pl.ANY)],
        out_specs=row(D),
        out_shape=jax.ShapeDtypeStruct((T, D), jnp.float32),
        scratch_shapes=[pltpu.VMEM((NE * PEER_ROWS, LANES), jnp.int32)] * (PEER_SETS * PEER_GROUP)
        + [pltpu.SemaphoreType.DMA((PEER_SETS * PEER_GROUP,))],
        compiler_params=_cparams(("arbitrary",)),
        name="peer_gather",
    )(ids.reshape(-1), g, h2, xm, gt, wtab)


def _pack_bf16_pairs(t):
    E, D = t.shape
    b = lax.bitcast_convert_type(t.astype(jnp.bfloat16), jnp.uint16).astype(jnp.uint32)
    w = b[:, :D // 2] | (b[:, D // 2:] << 16)
    return lax.bitcast_convert_type(w, jnp.int32).reshape(E, D // 2 // LANES, LANES)


def _rope_tables(pos, rot_dim, width):
    half = rot_dim // 2
    inv_freq = ROPE_THETA ** (-jnp.arange(half, dtype=jnp.float32) / half)
    ang = pos.astype(jnp.float32)[..., None] * inv_freq
    cos, sin = jnp.cos(ang), jnp.sin(ang)
    w = np.arange(LANES) % width
    first, second = w < half, (w >= half) & (w < rot_dim)
    src = np.where(first, w, np.where(second, w - half, 0))
    c = jnp.where(first | second, cos[:, src], 1.0)
    sa = jnp.where(first, -sin[:, src], 0.0)
    sb = jnp.where(second, sin[:, src], 0.0)
    return c, sa, sb


def _gmlp_mask():
    i = np.arange(GMLP_CHUNK)
    return (i[None, :] // CHUNK) <= (i[:, None] // CHUNK)


def _rows_per_seq(v, reps, tm):
    nb, d = v.shape
    return jnp.repeat(v, reps, axis=0).reshape(nb * reps // tm, tm, d)


def _layer(xp, xs, cache_k, cache_v, cache_kidx, c_prompt, c_sample, ada_w, ada_b, norm1_g, norm2_g, w_in,
           q_norm_g, k_norm_g, kidx_norm_g, gmlp_v_norm_g, gmlp_ws, gmlp_b, w_branch_a, w_branch_b, w_out,
           peer_wq, peer_subkeys, peer_u, peer_v):
    B, S, D = xp.shape
    DB, DS, _ = xs.shape
    P = cache_k.shape[1]
    Tp, Ts = B * S, DB * DS
    tm_p, tm_s = 2 * GMLP_CHUNK, GMLP_CHUNK
    assert S % tm_p == 0 and GMLP_CHUNK % DS == 0 and Ts % tm_s == 0 and DS <= CHUNK and P % CHUNK == 0
    tp_p, tp_s = min(PEER_TOK, S), min(PEER_TOK, Ts)
    assert S % tp_p == 0 and Ts % tp_s == 0
    assert tp_p % PEER_GROUP == 0 and tp_s % PEER_GROUP == 0 and min(tp_p, tp_s) // PEER_GROUP >= PEER_SETS
    bf = jnp.bfloat16
    f32 = jnp.float32

    nrow = B + DB
    npad = -nrow % SUBLANES
    c_all = jnp.concatenate([c_prompt, c_sample, jnp.zeros((npad, D), f32)], axis=0)
    mod = _adaln(c_all, ada_w, ada_b)
    mods = jnp.split(mod, 6, axis=-1)
    mp = [m[:B].reshape(B, 1, D) for m in mods]
    ms = [_rows_per_seq(m[B:B + DB], DS, tm_s) for m in mods]

    sizes = (GMLP_WIDTH, GMLP_WIDTH, N_HEADS * HEAD_DIM, N_KV_HEADS * HEAD_DIM, N_KV_HEADS * HEAD_DIM,
             N_IDX_HEADS * IDX_DIM, IDX_DIM, N_IDX_HEADS, 2 * D)
    pts = [int(s) for s in np.cumsum(sizes)[:-1]]
    wu, wv, wq, wk, wva, wqi, wki, wwi, wg = jnp.split(w_in.astype(bf), pts, axis=-1)
    wkw = jnp.concatenate([wki, wwi, jnp.zeros((D, LANES - IDX_DIM - N_IDX_HEADS), bf)], axis=-1)
    wts = (wu, wv, wq, wk, wva, wqi, wg, wkw)
    kig = jnp.concatenate([kidx_norm_g, jnp.zeros((LANES - IDX_DIM,), f32)]).reshape(1, LANES)
    gains = (q_norm_g.reshape(1, HEAD_DIM), k_norm_g.reshape(1, HEAD_DIM), kig, gmlp_v_norm_g.reshape(1, GMLP_WIDTH))
    g1 = norm1_g.reshape(1, D)
    g2 = norm2_g.reshape(1, D)

    wmask = jnp.where(_gmlp_mask()[None], gmlp_ws, 0.0)
    wm_p = wmask.astype(bf)
    gb_p = jnp.repeat(gmlp_b.T, GMLP_GROUP_DIM, axis=1)
    reps = GMLP_CHUNK // DS
    eye = jnp.eye(reps, dtype=f32)
    wm_s = jnp.einsum("ab,gij->gaibj", eye, wmask[:, :DS, :DS]).reshape(GMLP_GROUPS, GMLP_CHUNK, GMLP_CHUNK).astype(bf)
    gb_s = jnp.tile(gb_p[:DS], (reps, 1))

    pos_p = jnp.arange(S)
    pos_s = P + (jnp.arange(tm_s) % DS)
    tabs_p = _rope_tables(pos_p, ROT_DIM, HEAD_DIM) + _rope_tables(pos_p, IDX_ROT_DIM, IDX_DIM)
    tabs_s = _rope_tables(pos_s, ROT_DIM, HEAD_DIM) + _rope_tables(pos_s, IDX_ROT_DIM, IDX_DIM)

    x2p = xp.reshape(Tp, D)
    x2s = xs.reshape(Ts, D)

    (a_p, q_p, k32_p, k16_p, va32_p, va16_p, qi_p, kw32_p, ki16_p, sg_p) = _inproj(
        x2p, mp[1], mp[0], g1, wts, tabs_p, gains, wm_p, gb_p, tm=tm_p, tab_blocks=S // tm_p, emit_v=False)
    wit_p = kw32_p[:, IDX_DIM:IDX_DIM + N_IDX_HEADS].reshape(B, S, N_IDX_HEADS).transpose(0, 2, 1)
    vt_p = va16_p.reshape(B, S, -1).transpose(0, 2, 1)
    b_p = _dsa(qi_p, wit_p, q_p, ki16_p.reshape(B, S, LANES), k16_p.reshape(B, S, -1), vt_p,
               ksel=min(TOPK_MAX, S // 4), kt=min(KT_PROMPT, S), vis_step=QBLK, vis_base=QBLK,
               lim_lo=CHUNK, lim_hi=2 * CHUNK)

    (a_s, v_s, q_s, k32_s, k16_s, va32_s, va16_s, qi_s, kw32_s, ki16_s, sg_s) = _inproj(
        x2s, ms[1], ms[0], g1, wts, tabs_s, gains, wm_s, gb_s, tm=tm_s, tab_blocks=1, emit_v=True)
    Lk = P + DS
    Lpad = -(-Lk // KT_SAMPLE) * KT_SAMPLE

    def pad_q(a):
        w = a.shape[-1]
        return jnp.pad(a.reshape(DB, DS, w), ((0, 0), (0, QBLK - DS), (0, 0))).reshape(DB * QBLK, w)

    def cat_keys(cache, new, w):
        return jnp.concatenate([cache.reshape(DB, P, -1).astype(bf), new.reshape(DB, DS, -1),
                                jnp.zeros((DB, Lpad - Lk, w), bf)], axis=1)

    wit_s = jnp.pad(kw32_s[:, IDX_DIM:IDX_DIM + N_IDX_HEADS].reshape(DB, DS, N_IDX_HEADS).transpose(0, 2, 1),
                    ((0, 0), (0, 0), (0, QBLK - DS)))
    kidx_cache = jnp.pad(cache_kidx, ((0, 0), (0, 0), (0, LANES - IDX_DIM)))
    ki_all = cat_keys(kidx_cache, ki16_s, LANES)
    k_all = cat_keys(cache_k, k16_s, N_KV_HEADS * HEAD_DIM)
    vt_all = cat_keys(cache_v, va16_s, N_KV_HEADS * HEAD_DIM).transpose(0, 2, 1)
    b_s = _dsa(pad_q(qi_s), wit_s, pad_q(q_s), ki_all, k_all, vt_all,
               ksel=min(TOPK_MAX, Lk // 4), kt=KT_SAMPLE, vis_step=0, vis_base=Lpad, lim_lo=Lk, lim_hi=Lk)
    b_s = b_s.reshape(DB, QBLK, -1)[:, :DS].reshape(Ts, -1)

    wa, wb, wo, wpq = w_branch_a.astype(bf), w_branch_b.astype(bf), w_out.astype(bf), peer_wq.astype(bf)
    sk16 = peer_subkeys.astype(bf)
    wtab = jnp.concatenate([_pack_bf16_pairs(peer_u), _pack_bf16_pairs(peer_v)], axis=1).reshape(-1, LANES)

    def tail(a, b, sg, x2d, m, tm):
        xm, h2, qp = _merge(a, b, sg, x2d, m[2], m[4], m[3], g2, wa, wb, wo, wpq, tm=tm)
        e, g = _route(qp, sk16, tt=LANES)
        T = x2d.shape[0]
        ids = e.transpose(0, 2, 1).reshape(T, PEER_HEADS * PEER_TOPK) * PEER_ROWS
        gw = g.transpose(0, 2, 1).reshape(T, PEER_HEADS * PEER_TOPK)
        return ids, gw, h2, xm

    ids_p, gw_p, h2_p, xm_p = tail(a_p, b_p, sg_p, x2p, mp, tm_p)
    out_p = _peer(ids_p, gw_p, h2_p, xm_p, mp[5], wtab, tp=tp_p)
    ms_peer = [_rows_per_seq(m[B:B + DB], DS, tp_s) for m in mods]
    ids_s, gw_s, h2_s, xm_s = tail(a_s, b_s, sg_s, x2s, ms, tm_s)
    out_s = _peer(ids_s, gw_s, h2_s, xm_s, ms_peer[5], wtab, tp=tp_s)

    new = (k32_p.reshape(B, S, N_KV_HEADS, HEAD_DIM), va32_p.reshape(B, S, N_KV_HEADS, HEAD_DIM),
           kw32_p[:, :IDX_DIM].reshape(B, S, IDX_DIM),
           k32_s.reshape(DB, DS, N_KV_HEADS, HEAD_DIM), va32_s.reshape(DB, DS, N_KV_HEADS, HEAD_DIM),
           kw32_s[:, :IDX_DIM].reshape(DB, DS, IDX_DIM), v_s.reshape(DB, DS, GMLP_WIDTH))
    return out_p.reshape(B, S, D), out_s.reshape(DB, DS, D), new


def kernel(x_prompt, x_sample, cache_k, cache_v, cache_kidx, c_prompt, c_sample, ada_w, ada_b, norm1_g, norm2_g,
           w_in, q_norm_g, k_norm_g, kidx_norm_g, gmlp_v_norm_g, gmlp_ws, gmlp_b, w_branch_a, w_branch_b, w_out,
           peer_wq, peer_subkeys, peer_u, peer_v):
    xp, xs = x_prompt, x_sample
    per_layer = []
    for l in range(ada_w.shape[0]):
        xp, xs, new = _layer(xp, xs, cache_k[l], cache_v[l], cache_kidx[l], c_prompt, c_sample, ada_w[l], ada_b[l],
                             norm1_g[l], norm2_g[l], w_in[l], q_norm_g[l], k_norm_g[l], kidx_norm_g[l],
                             gmlp_v_norm_g[l], gmlp_ws[l], gmlp_b[l], w_branch_a[l], w_branch_b[l], w_out[l],
                             peer_wq[l], peer_subkeys[l], peer_u[l], peer_v[l])
        per_layer.append(new)
    stacked = tuple(jnp.stack([n[i] for n in per_layer]) for i in range(7))
    return (xp, xs) + stacked
```

```python
import functools

import jax
import jax.numpy as jnp
import numpy as np
from jax import lax
from jax.experimental import pallas as pl
from jax.experimental.pallas import tpu as pltpu

CHUNK = 64
EPS = 1e-6
ROPE_THETA = 500000.0
GMLP_CHUNK = 128
GMLP_GROUPS = 8
GMLP_GROUP_DIM = 128
GMLP_WIDTH = GMLP_GROUPS * GMLP_GROUP_DIM
N_HEADS = 8
N_KV_HEADS = 2
Q_PER_KV = N_HEADS // N_KV_HEADS
HEAD_DIM = 128
ROT_DIM = HEAD_DIM // 4
N_IDX_HEADS = 8
IDX_DIM = 64
IDX_ROT_DIM = IDX_DIM // 4
TOPK_MAX = 256
PEER_HEADS = 8
PEER_N_KEYS = 128
PEER_KEY_DIM = 256
PEER_TOPK = 16

LANES = 128
SUBLANES = 8
VMEM_LIMIT_BYTES = 56 * 1024 * 1024

QBLK = LANES
KT_PROMPT = 1024
KT_SAMPLE = 256
KT_COUNT = 512
QK_FOLD = HEAD_DIM ** -0.5 * float(np.log2(np.e))
PEER_TOK = 256
PEER_SETS = 3
PEER_GROUP = 4
PEER_ROWS = 8
INT_MIN = -(2 ** 31)
NEG_BIG = -1e30
IDX_BIG = 2 ** 30


def _cparams(sem):
    return pltpu.CompilerParams(dimension_semantics=sem, vmem_limit_bytes=VMEM_LIMIT_BYTES)


def _rms(x, width):
    return x * lax.rsqrt(jnp.sum(x * x, axis=-1, keepdims=True) * (1.0 / width) + EPS)


def _modulate(x, g, sc, sh):
    return _rms(x, x.shape[-1]) * g * (1.0 + sc) + sh


def _rope(x, c, sa, sb, half):
    n = x.shape[-1]
    return x * c + pltpu.roll(x, n - half, 1) * sa + pltpu.roll(x, half, 1) * sb


def _bdot(a, b):
    return jnp.dot(a, b, preferred_element_type=jnp.float32)


def _dot_nt(a, b):
    return lax.dot_general(a, b, (((1,), (1,)), ((), ())), preferred_element_type=jnp.float32)


def _adaln_kernel(c_ref, w_ref, b_ref, o_ref):
    c = c_ref[...]
    h = (c * jax.nn.sigmoid(c)).astype(jnp.bfloat16)
    o_ref[...] = _bdot(h, w_ref[...].astype(jnp.bfloat16)) + b_ref[...]


def _adaln(c, w, b):
    rows, d = c.shape
    n = w.shape[1]
    tn = n // 4
    return pl.pallas_call(
        _adaln_kernel,
        grid=(n // tn,),
        in_specs=[pl.BlockSpec((rows, d), lambda j: (0, 0)),
                  pl.BlockSpec((d, tn), lambda j: (0, j)),
                  pl.BlockSpec((1, tn), lambda j: (0, j))],
        out_specs=pl.BlockSpec((rows, tn), lambda j: (0, j)),
        out_shape=jax.ShapeDtypeStruct((rows, n), jnp.float32),
        compiler_params=_cparams(("arbitrary",)),
        name="adaln",
    )(c, w, b.reshape(1, n))


def _inproj_kernel(x_ref, sc_ref, sh_ref, g1_ref,
                   wu_ref, wv_ref, wq_ref, wk_ref, wva_ref, wqi_ref, wg_ref, wkw_ref,
                   c128_ref, sa128_ref, sb128_ref, c64_ref, sa64_ref, sb64_ref,
                   qg_ref, kg_ref, kig_ref, gvg_ref, wm_ref, gb_ref,
                   *out_refs, emit_v):
    if emit_v:
        (a_ref, v_ref, q_ref, k32_ref, k16_ref, va32_ref, va16_ref, qi_ref, kw32_ref, ki16_ref, sg_ref) = out_refs
    else:
        (a_ref, q_ref, k32_ref, k16_ref, va32_ref, va16_ref, qi_ref, kw32_ref, ki16_ref, sg_ref) = out_refs
        v_ref = None
    tm = x_ref.shape[0]
    h = _modulate(x_ref[...], g1_ref[...], sc_ref[...], sh_ref[...]).astype(jnp.bfloat16)

    c128, sa128, sb128 = c128_ref[...], sa128_ref[...], sb128_ref[...]
    c64, sa64, sb64 = c64_ref[...], sa64_ref[...], sb64_ref[...]

    v = _rms(jax.nn.gelu(_bdot(h, wv_ref[...])), GMLP_WIDTH) * gvg_ref[...]
    if v_ref is not None:
        v_ref[...] = v
    v16 = v.astype(jnp.bfloat16)
    u = jax.nn.gelu(_bdot(h, wu_ref[...]))
    for c in range(tm // GMLP_CHUNK):
        rows = slice(c * GMLP_CHUNK, (c + 1) * GMLP_CHUNK)
        for g in range(GMLP_GROUPS):
            cols = slice(g * GMLP_GROUP_DIM, (g + 1) * GMLP_GROUP_DIM)
            s = _bdot(wm_ref[g], v16[rows, cols]) + gb_ref[:, cols]
            a_ref[rows, cols] = (u[rows, cols] * s).astype(jnp.bfloat16)

    q = _bdot(h, wq_ref[...])
    for hd in range(N_HEADS):
        cols = slice(hd * HEAD_DIM, (hd + 1) * HEAD_DIM)
        qh = _rope(_rms(q[:, cols], HEAD_DIM) * qg_ref[...], c128, sa128, sb128, ROT_DIM // 2)
        q_ref[:, cols] = (qh * QK_FOLD).astype(jnp.bfloat16)
    k = _bdot(h, wk_ref[...])
    for hd in range(N_KV_HEADS):
        cols = slice(hd * HEAD_DIM, (hd + 1) * HEAD_DIM)
        kh = _rope(_rms(k[:, cols], HEAD_DIM) * kg_ref[...], c128, sa128, sb128, ROT_DIM // 2)
        k32_ref[:, cols] = kh
        k16_ref[:, cols] = kh.astype(jnp.bfloat16)
    va = _bdot(h, wva_ref[...])
    va32_ref[...] = va
    va16_ref[...] = va.astype(jnp.bfloat16)

    qi = _bdot(h, wqi_ref[...])
    for p in range(N_IDX_HEADS * IDX_DIM // LANES):
        cols = slice(p * LANES, (p + 1) * LANES)
        qi_ref[:, cols] = _rope(qi[:, cols], c64, sa64, sb64, IDX_ROT_DIM // 2).astype(jnp.bfloat16)
    kw = _bdot(h, wkw_ref[...])
    is_ki = lax.broadcasted_iota(jnp.int32, kw.shape, 1) < IDX_DIM
    kis = jnp.where(is_ki, kw, 0.0)
    kin = kis * lax.rsqrt(jnp.sum(kis * kis, axis=-1, keepdims=True) * (1.0 / IDX_DIM) + EPS) * kig_ref[...]
    kir = _rope(kin, c64, sa64, sb64, IDX_ROT_DIM // 2)
    kw32_ref[...] = jnp.where(is_ki, kir, kw * (N_IDX_HEADS ** -0.5))
    ki16_ref[...] = jnp.where(is_ki, kir, 0.0).astype(jnp.bfloat16)

    sg_ref[...] = jax.nn.sigmoid(_bdot(h, wg_ref[...])).astype(jnp.bfloat16)


def _inproj(x2d, sc, sh, g1, wts, tabs, gains, wm, gb, *, tm, tab_blocks, emit_v):
    T, D = x2d.shape
    nt = T // tm
    n_mod, R, _ = sc.shape
    tiles_per_mod = nt // n_mod
    const2 = lambda i: (0, 0)
    mod_spec = pl.BlockSpec((None, R, D), lambda i: (i // tiles_per_mod, 0, 0))
    tab_spec = pl.BlockSpec((tm, LANES), lambda i: (i % tab_blocks, 0))
    in_specs = ([pl.BlockSpec((tm, D), lambda i: (i, 0)), mod_spec, mod_spec, pl.BlockSpec((1, D), const2)]
                + [pl.BlockSpec(w.shape, const2) for w in wts]
                + [tab_spec] * 6
                + [pl.BlockSpec(g.shape, const2) for g in gains]
                + [pl.BlockSpec(wm.shape, lambda i: (0, 0, 0)), pl.BlockSpec(gb.shape, const2)])
    widths = [(GMLP_WIDTH, jnp.bfloat16)]
    if emit_v:
        widths.append((GMLP_WIDTH, jnp.float32))
    widths += [(N_HEADS * HEAD_DIM, jnp.bfloat16),
               (N_KV_HEADS * HEAD_DIM, jnp.float32), (N_KV_HEADS * HEAD_DIM, jnp.bfloat16),
               (N_KV_HEADS * HEAD_DIM, jnp.float32), (N_KV_HEADS * HEAD_DIM, jnp.bfloat16),
               (N_IDX_HEADS * IDX_DIM, jnp.bfloat16),
               (LANES, jnp.float32), (LANES, jnp.bfloat16),
               (2 * D, jnp.bfloat16)]
    return pl.pallas_call(
        functools.partial(_inproj_kernel, emit_v=emit_v),
        grid=(nt,),
        in_specs=in_specs,
        out_specs=[pl.BlockSpec((tm, w), lambda i: (i, 0)) for w, _ in widths],
        out_shape=[jax.ShapeDtypeStruct((T, w), dt) for w, dt in widths],
        compiler_params=_cparams(("parallel",)),
        name="inproj_v" if emit_v else "inproj",
    )(x2d, sc, sh, g1, *wts, *tabs, *gains, wm, gb)


def _dsa_kernel(qi_ref, wi_ref, q_ref, ki_ref, k_ref, vt_ref, o_ref,
                keys_ref, bias_ref, meff_ref, acc_ref,
                *, ksel, KT, vis_step, vis_base, lim_lo, lim_hi, q_valid):
    j = pl.program_id(1)
    L = ki_ref.shape[0]
    ntiles = jnp.minimum((j * vis_step + vis_base + KT - 1) // KT, L // KT)
    lane = lax.broadcasted_iota(jnp.int32, (1, QBLK), 1)
    limit = j * vis_step + jnp.where(lane < CHUNK, lim_lo, lim_hi)
    if q_valid < QBLK:
        limit = jnp.where(lane < q_valid, limit, 0)
    row = lax.broadcasted_iota(jnp.int32, (KT, QBLK), 0)

    def tile_off(t):
        return pl.multiple_of(t * KT, KT)

    wi = wi_ref[...] * (IDX_DIM ** -0.5)
    qit = qi_ref[...].astype(jnp.float32).T.astype(jnp.bfloat16)
    qit = jnp.concatenate([qit[h * IDX_DIM:(h + 1) * IDX_DIM, :] for h in range(N_IDX_HEADS)], axis=1)
    qit = jnp.concatenate([qit, jnp.zeros((LANES - IDX_DIM, N_IDX_HEADS * QBLK), jnp.bfloat16)], axis=0)

    def idx_body(t, carry):
        off = tile_off(t)
        s = _bdot(ki_ref[pl.ds(off, KT), :], qit)
        acc = jnp.zeros((KT, QBLK), jnp.float32)
        for h in range(N_IDX_HEADS):
            acc = acc + jnp.maximum(s[:, h * QBLK:(h + 1) * QBLK], 0.0) * wi[h:h + 1, :]
        bits = pltpu.bitcast(acc, jnp.int32)
        key = jnp.where(bits >= 0, bits, bits ^ 0x7FFFFFFF)
        keys_ref[pl.ds(off, KT), :] = jnp.where(row + off < limit, key, INT_MIN)
        return carry

    lax.fori_loop(0, ntiles, idx_body, 0)

    KC = min(KT, KT_COUNT)
    nctiles = jnp.minimum((j * vis_step + vis_base + KC - 1) // KC, L // KC)
    rowc = lax.broadcasted_iota(jnp.int32, (KC, QBLK), 0)

    def count(indicator):
        def body(t, acc):
            off = pl.multiple_of(t * KC, KC)
            ones = indicator(keys_ref[pl.ds(off, KC), :], off)
            return acc + ones.reshape(KC // SUBLANES, SUBLANES, QBLK).sum(axis=0)
        acc = lax.fori_loop(0, nctiles, body, jnp.zeros((SUBLANES, QBLK), jnp.int32))
        return acc.sum(axis=0, keepdims=True)

    def bit_body(i, tu):
        cand_u = tu | lax.shift_left(jnp.int32(1), 31 - i)
        cand_s = cand_u ^ INT_MIN
        cnt = count(lambda key, off: jnp.where(key >= cand_s, 1, 0))
        return jnp.where(cnt >= ksel, cand_u, tu)

    tu = lax.fori_loop(0, 32, bit_body, jnp.zeros((1, QBLK), jnp.int32))
    thr = jnp.maximum(tu ^ INT_MIN, INT_MIN + 1)

    need = count(lambda key, off: jnp.where(key >= thr, 1, 0)) > ksel
    meff_ref[...] = jnp.full(meff_ref.shape, IDX_BIG, jnp.int32)

    @pl.when(jnp.max(need.astype(jnp.int32)) > 0)
    def _():
        want = ksel - count(lambda key, off: jnp.where(key > thr, 1, 0))
        nbits = int(L).bit_length()

        def m_body(i, mp):
            cand = mp | lax.shift_left(jnp.int32(1), nbits - 1 - i)
            f = count(lambda key, off: jnp.where(key == thr, jnp.where(rowc + off < cand, 1, 0), 0))
            return jnp.where(f < want, cand, mp)

        mp = lax.fori_loop(0, nbits, m_body, jnp.zeros((1, QBLK), jnp.int32))
        meff_ref[...] = jnp.broadcast_to(jnp.where(need, mp, IDX_BIG), meff_ref.shape)

    meff = meff_ref[0:1, :]

    def bias_body(t, carry):
        off = tile_off(t)
        key = keys_ref[pl.ds(off, KT), :]
        tie = jnp.where(row + off <= meff, 0.0, NEG_BIG)
        bias_ref[pl.ds(off, KT), :] = jnp.where(key == thr, tie, jnp.where(key > thr, 0.0, NEG_BIG))
        return carry

    lax.fori_loop(0, ntiles, bias_body, 0)

    qgs = [jnp.concatenate([q_ref[:, (g * Q_PER_KV + h) * HEAD_DIM:(g * Q_PER_KV + h + 1) * HEAD_DIM]
                            for h in range(Q_PER_KV)], axis=0) for g in range(N_KV_HEADS)]
    acc_ref[...] = jnp.zeros(acc_ref.shape, jnp.float32)
    W = Q_PER_KV * QBLK

    def att_body(t, carry):
        off = tile_off(t)
        b = bias_ref[pl.ds(off, KT), :]
        b4 = jnp.concatenate([b] * Q_PER_KV, axis=1)
        out = []
        for g in range(N_KV_HEADS):
            m_old, l_old = carry[2 * g], carry[2 * g + 1]
            kt = k_ref[pl.ds(off, KT), g * HEAD_DIM:(g + 1) * HEAD_DIM]
            s = _dot_nt(kt, qgs[g]) + b4
            m_new = jnp.maximum(m_old, jnp.max(s, axis=0, keepdims=True))
            alpha = jnp.exp2(m_old - m_new)
            p = jnp.exp2(s - m_new)
            l_new = alpha * l_old + jnp.sum(p, axis=0, keepdims=True)
            vt = vt_ref[g * HEAD_DIM:(g + 1) * HEAD_DIM, pl.ds(off, KT)]
            acc_ref[g] = acc_ref[g] * alpha + _bdot(vt, p.astype(jnp.bfloat16))
            out += [m_new, l_new]
        return tuple(out)

    init = (jnp.full((1, W), NEG_BIG, jnp.float32), jnp.zeros((1, W), jnp.float32)) * N_KV_HEADS
    fin = lax.fori_loop(0, ntiles, att_body, init)
    for g in range(N_KV_HEADS):
        o = acc_ref[g] * (1.0 / fin[2 * g + 1])
        for h in range(Q_PER_KV):
            hd = g * Q_PER_KV + h
            o_ref[:, hd * HEAD_DIM:(hd + 1) * HEAD_DIM] = o[:, h * QBLK:(h + 1) * QBLK].T.astype(jnp.bfloat16)


def _dsa(qi, wit, q, ki16, k16, vt, *, ksel, kt, vis_step, vis_base, lim_lo, lim_hi, q_valid=QBLK):
    NB, L, _ = ki16.shape
    Sq = wit.shape[2]
    nq = Sq // QBLK
    assert L % kt == 0
    kern = functools.partial(_dsa_kernel, ksel=ksel, KT=kt, vis_step=vis_step, vis_base=vis_base,
                             lim_lo=lim_lo, lim_hi=lim_hi, q_valid=q_valid)
    return pl.pallas_call(
        kern,
        grid=(NB, nq),
        in_specs=[pl.BlockSpec((QBLK, qi.shape[1]), lambda b, j: (b * nq + j, 0)),
                  pl.BlockSpec((None, N_IDX_HEADS, QBLK), lambda b, j: (b, 0, j)),
                  pl.BlockSpec((QBLK, q.shape[1]), lambda b, j: (b * nq + j, 0)),
                  pl.BlockSpec((None, L, LANES), lambda b, j: (b, 0, 0)),
                  pl.BlockSpec((None, L, k16.shape[2]), lambda b, j: (b, 0, 0)),
                  pl.BlockSpec((None, vt.shape[1], L), lambda b, j: (b, 0, 0))],
        out_specs=pl.BlockSpec((QBLK, q.shape[1]), lambda b, j: (b * nq + j, 0)),
        out_shape=jax.ShapeDtypeStruct(q.shape, jnp.bfloat16),
        scratch_shapes=[pltpu.VMEM((L, QBLK), jnp.int32),
                        pltpu.VMEM((L, QBLK), jnp.float32),
                        pltpu.VMEM((SUBLANES, QBLK), jnp.int32),
                        pltpu.VMEM((N_KV_HEADS, HEAD_DIM, Q_PER_KV * QBLK), jnp.float32)],
        compiler_params=_cparams(("parallel", "arbitrary")),
        name="dsa",
    )(qi, wit, q, ki16, k16, vt)


def _merge_kernel(a_ref, b_ref, sg_ref, x_ref, gt_ref, sc2_ref, sh2_ref, g2_ref,
                  wa_ref, wb_ref, wo_ref, wq_ref, xm_ref, h2_ref, qp_ref):
    D = x_ref.shape[1]
    sg = sg_ref[...].astype(jnp.float32)
    merged = sg[:, :D] * _bdot(a_ref[...], wa_ref[...]) + sg[:, D:] * _bdot(b_ref[...], wb_ref[...])
    y = _bdot(merged.astype(jnp.bfloat16), wo_ref[...])
    xm = x_ref[...] + gt_ref[...] * y
    xm_ref[...] = xm
    h2 = _modulate(xm, g2_ref[...], sc2_ref[...], sh2_ref[...])
    h2_ref[...] = h2
    qp_ref[...] = _bdot(h2.astype(jnp.bfloat16), wq_ref[...]).astype(jnp.bfloat16)


def _merge(a, b, sg, x2d, gt, sc2, sh2, g2, wa, wb, wo, wq, *, tm):
    T, D = x2d.shape
    nt = T // tm
    n_mod, R, _ = gt.shape
    tiles_per_mod = nt // n_mod
    const2 = lambda i: (0, 0)
    row = lambda w: pl.BlockSpec((tm, w), lambda i: (i, 0))
    mod_spec = pl.BlockSpec((None, R, D), lambda i: (i // tiles_per_mod, 0, 0))
    nq = wq.shape[1]
    return pl.pallas_call(
        _merge_kernel,
        grid=(nt,),
        in_specs=[row(a.shape[1]), row(b.shape[1]), row(sg.shape[1]), row(D), mod_spec, mod_spec, mod_spec,
                  pl.BlockSpec((1, D), const2)] + [pl.BlockSpec(w.shape, const2) for w in (wa, wb, wo, wq)],
        out_specs=[row(D), row(D), row(nq)],
        out_shape=[jax.ShapeDtypeStruct((T, D), jnp.float32), jax.ShapeDtypeStruct((T, D), jnp.float32),
                   jax.ShapeDtypeStruct((T, nq), jnp.bfloat16)],
        compiler_params=_cparams(("parallel",)),
        name="merge",
    )(a, b, sg, x2d, gt, sc2, sh2, g2, wa, wb, wo, wq)


_PAIR_COUNTS = [PEER_TOPK // (a + 1) for a in range(PEER_TOPK)]
_PAIR_OFFS = [int(v) for v in np.cumsum([0] + _PAIR_COUNTS[:-1])]
_N_PAIRS = sum(_PAIR_COUNTS)
_PAIR_ROWS = -(-_N_PAIRS // SUBLANES) * SUBLANES


def _route_kernel(qp_ref, sk_ref, e_ref, g_ref, s_ref, sv_ref, si_ref, cand_ref, pay_ref, gv_ref):
    TT = qp_ref.shape[0]
    half = PEER_KEY_DIM // 2
    NC = PEER_HEADS * 2
    NEG_INF = -jnp.inf

    for c in range(NC):
        s_ref[c] = _dot_nt(sk_ref[c // 2, c % 2], qp_ref[:, c * half:(c + 1) * half])

    rowk = lax.broadcasted_iota(jnp.int32, (PEER_N_KEYS, TT), 0)

    def key_round(r, carry):
        for c in range(NC):
            s = s_ref[c]
            m = jnp.max(s, axis=0, keepdims=True)
            am = jnp.min(jnp.where(s == m, rowk, PEER_N_KEYS), axis=0, keepdims=True)
            sv_ref[c, pl.ds(r, 1), :] = m
            si_ref[c, pl.ds(r, 1), :] = am
            s_ref[c] = jnp.where(rowk == am, NEG_INF, s)
        return carry

    lax.fori_loop(0, PEER_TOPK, key_round, 0)

    for h in range(PEER_HEADS):
        sv0, sv1 = sv_ref[2 * h], sv_ref[2 * h + 1]
        si0, si1 = si_ref[2 * h], si_ref[2 * h + 1]
        cand_ref[h, _PAIR_ROWS - SUBLANES:_PAIR_ROWS, :] = jnp.full((SUBLANES, TT), NEG_INF, jnp.float32)
        pay_ref[h, _PAIR_ROWS - SUBLANES:_PAIR_ROWS, :] = jnp.zeros((SUBLANES, TT), jnp.int32)
        for a in range(PEER_TOPK):
            n, off = _PAIR_COUNTS[a], _PAIR_OFFS[a]
            cand_ref[h, off:off + n, :] = sv0[a:a + 1, :] + sv1[0:n, :]
            pay_ref[h, off:off + n, :] = si0[a:a + 1, :] * PEER_N_KEYS + si1[0:n, :]

    rowp = lax.broadcasted_iota(jnp.int32, (_PAIR_ROWS, TT), 0)

    def pair_round(r, carry):
        for h in range(PEER_HEADS):
            s = cand_ref[h]
            m = jnp.max(s, axis=0, keepdims=True)
            am = jnp.min(jnp.where(s == m, rowp, _PAIR_ROWS), axis=0, keepdims=True)
            hit = rowp == am
            gv_ref[h, pl.ds(r, 1), :] = m
            e_ref[pl.ds(h * PEER_TOPK + r, 1), :] = jnp.sum(jnp.where(hit, pay_ref[h], 0), axis=0, keepdims=True)
            cand_ref[h] = jnp.where(hit, NEG_INF, s)
        return carry

    lax.fori_loop(0, PEER_TOPK, pair_round, 0)

    for h in range(PEER_HEADS):
        gv = gv_ref[h]
        ex = jnp.exp(gv - jnp.max(gv, axis=0, keepdims=True))
        g_ref[h * PEER_TOPK:(h + 1) * PEER_TOPK, :] = ex / jnp.sum(ex, axis=0, keepdims=True)


def _route(qp, sk16, *, tt):
    T = qp.shape[0]
    nt = T // tt
    ne = PEER_HEADS * PEER_TOPK
    out = jax.ShapeDtypeStruct((nt, ne, tt), jnp.int32)
    blk = pl.BlockSpec((None, ne, tt), lambda i: (i, 0, 0))
    return pl.pallas_call(
        _route_kernel,
        grid=(nt,),
        in_specs=[pl.BlockSpec((tt, qp.shape[1]), lambda i: (i, 0)),
                  pl.BlockSpec(sk16.shape, lambda i: (0, 0, 0, 0))],
        out_specs=[blk, blk],
        out_shape=[out, jax.ShapeDtypeStruct(out.shape, jnp.float32)],
        scratch_shapes=[pltpu.VMEM((PEER_HEADS * 2, PEER_N_KEYS, tt), jnp.float32),
                        pltpu.VMEM((PEER_HEADS * 2, PEER_TOPK, tt), jnp.float32),
                        pltpu.VMEM((PEER_HEADS * 2, PEER_TOPK, tt), jnp.int32),
                        pltpu.VMEM((PEER_HEADS, _PAIR_ROWS, tt), jnp.float32),
                        pltpu.VMEM((PEER_HEADS, _PAIR_ROWS, tt), jnp.int32),
                        pltpu.VMEM((PEER_HEADS, PEER_TOPK, tt), jnp.float32)],
        compiler_params=_cparams(("parallel",)),
        name="peer_route",
    )(qp, sk16)


def _peer_kernel(ids_ref, g_ref, h2_ref, xm_ref, gt_ref, w_hbm, o_ref, *scratch):
    TP, D = h2_ref.shape
    NE = g_ref.shape[1]
    HS = D // 2 // LANES
    NG = TP // PEER_GROUP
    bf = jnp.bfloat16
    wbufs, sem = scratch[:-1], scratch[-1]

    def issue(grp, half):
        for p in range(PEER_GROUP):
            base = (grp * PEER_GROUP + p) * NE
            slot = half * PEER_GROUP + p
            for k in range(NE):
                src = pl.multiple_of(ids_ref[base + k], PEER_ROWS)
                pltpu.make_async_copy(w_hbm.at[pl.ds(src, PEER_ROWS)],
                                      wbufs[slot].at[pl.ds(k * PEER_ROWS, PEER_ROWS)],
                                      sem.at[slot]).start(priority=k % 2)

    def wait(slot):
        pltpu.make_async_copy(w_hbm.at[pl.ds(0, NE * PEER_ROWS)], wbufs[slot], sem.at[slot]).wait()

    def table(slot, r0):
        lo, hi = [], []
        for s in range(HS):
            w = wbufs[slot][pl.ds(r0 + s, NE, stride=PEER_ROWS), :]
            lo.append(pltpu.bitcast(w << 16, jnp.float32).astype(bf))
            hi.append(pltpu.bitcast(w & jnp.int32(-65536), jnp.float32).astype(bf))
        return jnp.concatenate(lo + hi, axis=1)

    for g0 in range(PEER_SETS - 1):
        issue(g0, g0)

    def ring(i, carry):
        for r in range(PEER_SETS):
            group(PEER_SETS * i + r, r, True)
        return carry

    def group(grp, half, issue_ahead):
        slots = [half * PEER_GROUP + p for p in range(PEER_GROUP)]
        toks = [grp * PEER_GROUP + p for p in range(PEER_GROUP)]
        for slot in slots:
            wait(slot)
        if issue_ahead:
            issue(grp + PEER_SETS - 1, (half + PEER_SETS - 1) % PEER_SETS)
        acts = []
        for t, slot in zip(toks, slots):
            x = jnp.broadcast_to(h2_ref[pl.ds(t, 1), :], (SUBLANES, D)).astype(bf)
            acts.append(_dot_nt(x, table(slot, 0))[0:1, :])
        ys = []
        for t, slot, act in zip(toks, slots, acts):
            c = g_ref[pl.ds(t, 1), :] * jax.nn.gelu(act)
            c8 = jnp.broadcast_to(c, (SUBLANES, NE)).astype(bf)
            ys.append(_bdot(c8, table(slot, HS))[0:1, :])
        for t, y in zip(toks, ys):
            gt = gt_ref[pl.ds(t, 1), :] if gt_ref.shape[0] == TP else gt_ref[...]
            o_ref[pl.ds(t, 1), :] = xm_ref[pl.ds(t, 1), :] + gt * y

    n_ring = (NG - (PEER_SETS - 1)) // PEER_SETS
    lax.fori_loop(0, n_ring, ring, 0)
    for grp in range(n_ring * PEER_SETS, NG):
        group(grp, grp % PEER_SETS, grp + PEER_SETS - 1 < NG)


def _peer(ids, g, h2, xm, gt, wtab, *, tp):
    T, D = h2.shape
    NE = ids.shape[1]
    nt = T // tp
    n_mod, R, _ = gt.shape
    tiles_per_mod = nt // n_mod
    row = lambda w: pl.BlockSpec((tp, w), lambda i: (i, 0))
    return pl.pallas_call(
        _peer_kernel,
        grid=(nt,),
        in_specs=[pl.BlockSpec((tp * NE,), lambda i: (i,), memory_space=pltpu.SMEM),
                  row(NE), row(D), row(D),
                  pl.BlockSpec((None, R, D), lambda i: (i // tiles_per_mod, 0, 0)),
                  pl.BlockSpec(memory_space=pl.ANY)],
        out_specs=row(D),
        out_shape=jax.ShapeDtypeStruct((T, D), jnp.float32),
        scratch_shapes=[pltpu.VMEM((NE * PEER_ROWS, LANES), jnp.int32)] * (PEER_SETS * PEER_GROUP)
        + [pltpu.SemaphoreType.DMA((PEER_SETS * PEER_GROUP,))],
        compiler_params=_cparams(("arbitrary",)),
        name="peer_gather",
    )(ids.reshape(-1), g, h2, xm, gt, wtab)


def _pack_bf16_pairs(t):
    E, D = t.shape
    b = lax.bitcast_convert_type(t.astype(jnp.bfloat16), jnp.uint16).astype(jnp.uint32)
    w = b[:, :D // 2] | (b[:, D // 2:] << 16)
    return lax.bitcast_convert_type(w, jnp.int32).reshape(E, D // 2 // LANES, LANES)


def _rope_tables(pos, rot_dim, width):
    half = rot_dim // 2
    inv_freq = ROPE_THETA ** (-jnp.arange(half, dtype=jnp.float32) / half)
    ang = pos.astype(jnp.float32)[..., None] * inv_freq
    cos, sin = jnp.cos(ang), jnp.sin(ang)
    w = np.arange(LANES) % width
    first, second = w < half, (w >= half) & (w < rot_dim)
    src = np.where(first, w, np.where(second, w - half, 0))
    c = jnp.where(first | second, cos[:, src], 1.0)
    sa = jnp.where(first, -sin[:, src], 0.0)
    sb = jnp.where(second, sin[:, src], 0.0)
    return c, sa, sb


def _gmlp_mask():
    i = np.arange(GMLP_CHUNK)
    return (i[None, :] // CHUNK) <= (i[:, None] // CHUNK)


def _rows_per_seq(v, reps, tm):
    nb, d = v.shape
    return jnp.repeat(v, reps, axis=0).reshape(nb * reps // tm, tm, d)


def _layer(xp, xs, cache_k, cache_v, cache_kidx, c_prompt, c_sample, ada_w, ada_b, norm1_g, norm2_g, w_in,
           q_norm_g, k_norm_g, kidx_norm_g, gmlp_v_norm_g, gmlp_ws, gmlp_b, w_branch_a, w_branch_b, w_out,
           peer_wq, peer_subkeys, peer_u, peer_v):
    B, S, D = xp.shape
    DB, DS, _ = xs.shape
    P = cache_k.shape[1]
    Tp, Ts = B * S, DB * DS
    tm_p, tm_s = 2 * GMLP_CHUNK, GMLP_CHUNK
    assert S % tm_p == 0 and GMLP_CHUNK % DS == 0 and Ts % tm_s == 0 and DS <= CHUNK and P % CHUNK == 0
    tp_p, tp_s = min(PEER_TOK, S), min(PEER_TOK, Ts)
    assert S % tp_p == 0 and Ts % tp_s == 0
    assert tp_p % PEER_GROUP == 0 and tp_s % PEER_GROUP == 0 and min(tp_p, tp_s) // PEER_GROUP >= PEER_SETS
    bf = jnp.bfloat16
    f32 = jnp.float32

    nrow = B + DB
    npad = -nrow % SUBLANES
    c_all = jnp.concatenate([c_prompt, c_sample, jnp.zeros((npad, D), f32)], axis=0)
    mod = _adaln(c_all, ada_w, ada_b)
    mods = jnp.split(mod, 6, axis=-1)
    mp = [m[:B].reshape(B, 1, D) for m in mods]
    ms = [_rows_per_seq(m[B:B + DB], DS, tm_s) for m in mods]

    sizes = (GMLP_WIDTH, GMLP_WIDTH, N_HEADS * HEAD_DIM, N_KV_HEADS * HEAD_DIM, N_KV_HEADS * HEAD_DIM,
             N_IDX_HEADS * IDX_DIM, IDX_DIM, N_IDX_HEADS, 2 * D)
    pts = [int(s) for s in np.cumsum(sizes)[:-1]]
    wu, wv, wq, wk, wva, wqi, wki, wwi, wg = jnp.split(w_in.astype(bf), pts, axis=-1)
    wkw = jnp.concatenate([wki, wwi, jnp.zeros((D, LANES - IDX_DIM - N_IDX_HEADS), bf)], axis=-1)
    wts = (wu, wv, wq, wk, wva, wqi, wg, wkw)
    kig = jnp.concatenate([kidx_norm_g, jnp.zeros((LANES - IDX_DIM,), f32)]).reshape(1, LANES)
    gains = (q_norm_g.reshape(1, HEAD_DIM), k_norm_g.reshape(1, HEAD_DIM), kig, gmlp_v_norm_g.reshape(1, GMLP_WIDTH))
    g1 = norm1_g.reshape(1, D)
    g2 = norm2_g.reshape(1, D)

    wmask = jnp.where(_gmlp_mask()[None], gmlp_ws, 0.0)
    wm_p = wmask.astype(bf)
    gb_p = jnp.repeat(gmlp_b.T, GMLP_GROUP_DIM, axis=1)
    reps = GMLP_CHUNK // DS
    eye = jnp.eye(reps, dtype=f32)
    wm_s = jnp.einsum("ab,gij->gaibj", eye, wmask[:, :DS, :DS]).reshape(GMLP_GROUPS, GMLP_CHUNK, GMLP_CHUNK).astype(bf)
    gb_s = jnp.tile(gb_p[:DS], (reps, 1))

    pos_p = jnp.arange(S)
    pos_s = P + (jnp.arange(tm_s) % DS)
    tabs_p = _rope_tables(pos_p, ROT_DIM, HEAD_DIM) + _rope_tables(pos_p, IDX_ROT_DIM, IDX_DIM)
    tabs_s = _rope_tables(pos_s, ROT_DIM, HEAD_DIM) + _rope_tables(pos_s, IDX_ROT_DIM, IDX_DIM)

    x2p = xp.reshape(Tp, D)
    x2s = xs.reshape(Ts, D)

    (a_p, q_p, k32_p, k16_p, va32_p, va16_p, qi_p, kw32_p, ki16_p, sg_p) = _inproj(
        x2p, mp[1], mp[0], g1, wts, tabs_p, gains, wm_p, gb_p, tm=tm_p, tab_blocks=S // tm_p, emit_v=False)
    wit_p = kw32_p[:, IDX_DIM:IDX_DIM + N_IDX_HEADS].reshape(B, S, N_IDX_HEADS).transpose(0, 2, 1)
    vt_p = va16_p.reshape(B, S, -1).transpose(0, 2, 1)
    b_p = _dsa(qi_p, wit_p, q_p, ki16_p.reshape(B, S, LANES), k16_p.reshape(B, S, -1), vt_p,
               ksel=min(TOPK_MAX, S // 4), kt=min(KT_PROMPT, S), vis_step=QBLK, vis_base=QBLK,
               lim_lo=CHUNK, lim_hi=2 * CHUNK)

    (a_s, v_s, q_s, k32_s, k16_s, va32_s, va16_s, qi_s, kw32_s, ki16_s, sg_s) = _inproj(
        x2s, ms[1], ms[0], g1, wts, tabs_s, gains, wm_s, gb_s, tm=tm_s, tab_blocks=1, emit_v=True)
    Lk = P + DS
    Lpad = -(-Lk // KT_SAMPLE) * KT_SAMPLE

    def pad_q(a):
        w = a.shape[-1]
        return jnp.pad(a.reshape(DB, DS, w), ((0, 0), (0, QBLK - DS), (0, 0))).reshape(DB * QBLK, w)

    def cat_keys(cache, new, w):
        return jnp.concatenate([cache.reshape(DB, P, -1).astype(bf), new.reshape(DB, DS, -1),
                                jnp.zeros((DB, Lpad - Lk, w), bf)], axis=1)

    wit_s = jnp.pad(kw32_s[:, IDX_DIM:IDX_DIM + N_IDX_HEADS].reshape(DB, DS, N_IDX_HEADS).transpose(0, 2, 1),
                    ((0, 0), (0, 0), (0, QBLK - DS)))
    kidx_cache = jnp.pad(cache_kidx, ((0, 0), (0, 0), (0, LANES - IDX_DIM)))
    ki_all = cat_keys(kidx_cache, ki16_s, LANES)
    k_all = cat_keys(cache_k, k16_s, N_KV_HEADS * HEAD_DIM)
    vt_all = cat_keys(cache_v, va16_s, N_KV_HEADS * HEAD_DIM).transpose(0, 2, 1)
    b_s = _dsa(pad_q(qi_s), wit_s, pad_q(q_s), ki_all, k_all, vt_all,
               ksel=min(TOPK_MAX, Lk // 4), kt=KT_SAMPLE, vis_step=0, vis_base=Lpad, lim_lo=Lk, lim_hi=Lk,
               q_valid=DS)
    b_s = b_s.reshape(DB, QBLK, -1)[:, :DS].reshape(Ts, -1)

    wa, wb, wo, wpq = w_branch_a.astype(bf), w_branch_b.astype(bf), w_out.astype(bf), peer_wq.astype(bf)
    sk16 = peer_subkeys.astype(bf)
    wtab = jnp.concatenate([_pack_bf16_pairs(peer_u), _pack_bf16_pairs(peer_v)], axis=1).reshape(-1, LANES)

    def tail(a, b, sg, x2d, m, tm):
        xm, h2, qp = _merge(a, b, sg, x2d, m[2], m[4], m[3], g2, wa, wb, wo, wpq, tm=tm)
        e, g = _route(qp, sk16, tt=LANES)
        T = x2d.shape[0]
        ids = e.transpose(0, 2, 1).reshape(T, PEER_HEADS * PEER_TOPK) * PEER_ROWS
        gw = g.transpose(0, 2, 1).reshape(T, PEER_HEADS * PEER_TOPK)
        return ids, gw, h2, xm

    ids_p, gw_p, h2_p, xm_p = tail(a_p, b_p, sg_p, x2p, mp, tm_p)
    out_p = _peer(ids_p, gw_p, h2_p, xm_p, mp[5], wtab, tp=tp_p)
    ms_peer = [_rows_per_seq(m[B:B + DB], DS, tp_s) for m in mods]
    ids_s, gw_s, h2_s, xm_s = tail(a_s, b_s, sg_s, x2s, ms, tm_s)
    out_s = _peer(ids_s, gw_s, h2_s, xm_s, ms_peer[5], wtab, tp=tp_s)

    new = (k32_p.reshape(B, S, N_KV_HEADS, HEAD_DIM), va32_p.reshape(B, S, N_KV_HEADS, HEAD_DIM),
           kw32_p[:, :IDX_DIM].reshape(B, S, IDX_DIM),
           k32_s.reshape(DB, DS, N_KV_HEADS, HEAD_DIM), va32_s.reshape(DB, DS, N_KV_HEADS, HEAD_DIM),
           kw32_s[:, :IDX_DIM].reshape(DB, DS, IDX_DIM), v_s.reshape(DB, DS, GMLP_WIDTH))
    return out_p.reshape(B, S, D), out_s.reshape(DB, DS, D), new


def kernel(x_prompt, x_sample, cache_k, cache_v, cache_kidx, c_prompt, c_sample, ada_w, ada_b, norm1_g, norm2_g,
           w_in, q_norm_g, k_norm_g, kidx_norm_g, gmlp_v_norm_g, gmlp_ws, gmlp_b, w_branch_a, w_branch_b, w_out,
           peer_wq, peer_subkeys, peer_u, peer_v):
    xp, xs = x_prompt, x_sample
    per_layer = []
    for l in range(ada_w.shape[0]):
        xp, xs, new = _layer(xp, xs, cache_k[l], cache_v[l], cache_kidx[l], c_prompt, c_sample, ada_w[l], ada_b[l],
                             norm1_g[l], norm2_g[l], w_in[l], q_norm_g[l], k_norm_g[l], kidx_norm_g[l],
                             gmlp_v_norm_g[l], gmlp_ws[l], gmlp_b[l], w_branch_a[l], w_branch_b[l], w_out[l],
                             peer_wq[l], peer_subkeys[l], peer_u[l], peer_v[l])
        per_layer.append(new)
    stacked = tuple(jnp.stack([n[i] for n in per_layer]) for i in range(7))
    return (xp, xs) + stacked
```

```python
import functools

import jax
import jax.numpy as jnp
import numpy as np
from jax import lax
from jax.experimental import pallas as pl
from jax.experimental.pallas import tpu as pltpu

CHUNK = 64
EPS = 1e-6
ROPE_THETA = 500000.0
GMLP_CHUNK = 128
GMLP_GROUPS = 8
GMLP_GROUP_DIM = 128
GMLP_WIDTH = GMLP_GROUPS * GMLP_GROUP_DIM
N_HEADS = 8
N_KV_HEADS = 2
Q_PER_KV = N_HEADS // N_KV_HEADS
HEAD_DIM = 128
ROT_DIM = HEAD_DIM // 4
N_IDX_HEADS = 8
IDX_DIM = 64
IDX_ROT_DIM = IDX_DIM // 4
TOPK_MAX = 256
PEER_HEADS = 8
PEER_N_KEYS = 128
PEER_KEY_DIM = 256
PEER_TOPK = 16

LANES = 128
SUBLANES = 8
VMEM_LIMIT_BYTES = 56 * 1024 * 1024

QBLK = LANES
KT_PROMPT = 1024
KT_SAMPLE = 256
KT_COUNT = 512
QK_FOLD = HEAD_DIM ** -0.5 * float(np.log2(np.e))
SOFTMAX_CEIL_MAX = 50.0
PEER_TOK = 256
PEER_SETS = 3
PEER_GROUP = 4
PEER_ROWS = 8
INT_MIN = -(2 ** 31)
NEG_BIG = -1e30
IDX_BIG = 2 ** 30


def _cparams(sem):
    return pltpu.CompilerParams(dimension_semantics=sem, vmem_limit_bytes=VMEM_LIMIT_BYTES)


def _rms(x, width):
    return x * lax.rsqrt(jnp.sum(x * x, axis=-1, keepdims=True) * (1.0 / width) + EPS)


def _modulate(x, g, sc, sh):
    return _rms(x, x.shape[-1]) * g * (1.0 + sc) + sh


def _rope(x, c, sa, sb, half):
    n = x.shape[-1]
    return x * c + pltpu.roll(x, n - half, 1) * sa + pltpu.roll(x, half, 1) * sb


def _bdot(a, b):
    return jnp.dot(a, b, preferred_element_type=jnp.float32)


def _dot_nt(a, b):
    return lax.dot_general(a, b, (((1,), (1,)), ((), ())), preferred_element_type=jnp.float32)


def _adaln_kernel(c_ref, w_ref, b_ref, o_ref):
    c = c_ref[...]
    h = (c * jax.nn.sigmoid(c)).astype(jnp.bfloat16)
    o_ref[...] = _bdot(h, w_ref[...].astype(jnp.bfloat16)) + b_ref[...]


def _adaln(c, w, b):
    rows, d = c.shape
    n = w.shape[1]
    tn = n // 4
    return pl.pallas_call(
        _adaln_kernel,
        grid=(n // tn,),
        in_specs=[pl.BlockSpec((rows, d), lambda j: (0, 0)),
                  pl.BlockSpec((d, tn), lambda j: (0, j)),
                  pl.BlockSpec((1, tn), lambda j: (0, j))],
        out_specs=pl.BlockSpec((rows, tn), lambda j: (0, j)),
        out_shape=jax.ShapeDtypeStruct((rows, n), jnp.float32),
        compiler_params=_cparams(("arbitrary",)),
        name="adaln",
    )(c, w, b.reshape(1, n))


def _inproj_kernel(x_ref, sc_ref, sh_ref, g1_ref,
                   wu_ref, wv_ref, wq_ref, wk_ref, wva_ref, wqi_ref, wg_ref, wkw_ref,
                   c128_ref, sa128_ref, sb128_ref, c64_ref, sa64_ref, sb64_ref,
                   qg_ref, kg_ref, kig_ref, gvg_ref, wm_ref, gb_ref,
                   *out_refs, emit_v):
    if emit_v:
        (a_ref, v_ref, q_ref, k32_ref, k16_ref, va32_ref, va16_ref, qi_ref, kw32_ref, ki16_ref, sg_ref) = out_refs
    else:
        (a_ref, q_ref, k32_ref, k16_ref, va32_ref, va16_ref, qi_ref, kw32_ref, ki16_ref, sg_ref) = out_refs
        v_ref = None
    tm = x_ref.shape[0]
    h = _modulate(x_ref[...], g1_ref[...], sc_ref[...], sh_ref[...]).astype(jnp.bfloat16)

    c128, sa128, sb128 = c128_ref[...], sa128_ref[...], sb128_ref[...]
    c64, sa64, sb64 = c64_ref[...], sa64_ref[...], sb64_ref[...]

    v = _rms(jax.nn.gelu(_bdot(h, wv_ref[...])), GMLP_WIDTH) * gvg_ref[...]
    if v_ref is not None:
        v_ref[...] = v
    v16 = v.astype(jnp.bfloat16)
    u = jax.nn.gelu(_bdot(h, wu_ref[...]))
    for c in range(tm // GMLP_CHUNK):
        rows = slice(c * GMLP_CHUNK, (c + 1) * GMLP_CHUNK)
        for g in range(GMLP_GROUPS):
            cols = slice(g * GMLP_GROUP_DIM, (g + 1) * GMLP_GROUP_DIM)
            s = _bdot(wm_ref[g], v16[rows, cols]) + gb_ref[:, cols]
            a_ref[rows, cols] = (u[rows, cols] * s).astype(jnp.bfloat16)

    q = _bdot(h, wq_ref[...])
    for hd in range(N_HEADS):
        cols = slice(hd * HEAD_DIM, (hd + 1) * HEAD_DIM)
        qh = _rope(_rms(q[:, cols], HEAD_DIM) * qg_ref[...], c128, sa128, sb128, ROT_DIM // 2)
        q_ref[:, cols] = (qh * QK_FOLD).astype(jnp.bfloat16)
    k = _bdot(h, wk_ref[...])
    for hd in range(N_KV_HEADS):
        cols = slice(hd * HEAD_DIM, (hd + 1) * HEAD_DIM)
        kh = _rope(_rms(k[:, cols], HEAD_DIM) * kg_ref[...], c128, sa128, sb128, ROT_DIM // 2)
        k32_ref[:, cols] = kh
        k16_ref[:, cols] = kh.astype(jnp.bfloat16)
    va = _bdot(h, wva_ref[...])
    va32_ref[...] = va
    va16_ref[...] = va.astype(jnp.bfloat16)

    qi = _bdot(h, wqi_ref[...])
    for p in range(N_IDX_HEADS * IDX_DIM // LANES):
        cols = slice(p * LANES, (p + 1) * LANES)
        qi_ref[:, cols] = _rope(qi[:, cols], c64, sa64, sb64, IDX_ROT_DIM // 2).astype(jnp.bfloat16)
    kw = _bdot(h, wkw_ref[...])
    is_ki = lax.broadcasted_iota(jnp.int32, kw.shape, 1) < IDX_DIM
    kis = jnp.where(is_ki, kw, 0.0)
    kin = kis * lax.rsqrt(jnp.sum(kis * kis, axis=-1, keepdims=True) * (1.0 / IDX_DIM) + EPS) * kig_ref[...]
    kir = _rope(kin, c64, sa64, sb64, IDX_ROT_DIM // 2)
    kw32_ref[...] = jnp.where(is_ki, kir, kw * (N_IDX_HEADS ** -0.5))
    ki16_ref[...] = jnp.where(is_ki, kir, 0.0).astype(jnp.bfloat16)

    sg_ref[...] = jax.nn.sigmoid(_bdot(h, wg_ref[...])).astype(jnp.bfloat16)


def _inproj(x2d, sc, sh, g1, wts, tabs, gains, wm, gb, *, tm, tab_blocks, emit_v):
    T, D = x2d.shape
    nt = T // tm
    n_mod, R, _ = sc.shape
    tiles_per_mod = nt // n_mod
    const2 = lambda i: (0, 0)
    mod_spec = pl.BlockSpec((None, R, D), lambda i: (i // tiles_per_mod, 0, 0))
    tab_spec = pl.BlockSpec((tm, LANES), lambda i: (i % tab_blocks, 0))
    in_specs = ([pl.BlockSpec((tm, D), lambda i: (i, 0)), mod_spec, mod_spec, pl.BlockSpec((1, D), const2)]
                + [pl.BlockSpec(w.shape, const2) for w in wts]
                + [tab_spec] * 6
                + [pl.BlockSpec(g.shape, const2) for g in gains]
                + [pl.BlockSpec(wm.shape, lambda i: (0, 0, 0)), pl.BlockSpec(gb.shape, const2)])
    widths = [(GMLP_WIDTH, jnp.bfloat16)]
    if emit_v:
        widths.append((GMLP_WIDTH, jnp.float32))
    widths += [(N_HEADS * HEAD_DIM, jnp.bfloat16),
               (N_KV_HEADS * HEAD_DIM, jnp.float32), (N_KV_HEADS * HEAD_DIM, jnp.bfloat16),
               (N_KV_HEADS * HEAD_DIM, jnp.float32), (N_KV_HEADS * HEAD_DIM, jnp.bfloat16),
               (N_IDX_HEADS * IDX_DIM, jnp.bfloat16),
               (LANES, jnp.float32), (LANES, jnp.bfloat16),
               (2 * D, jnp.bfloat16)]
    return pl.pallas_call(
        functools.partial(_inproj_kernel, emit_v=emit_v),
        grid=(nt,),
        in_specs=in_specs,
        out_specs=[pl.BlockSpec((tm, w), lambda i: (i, 0)) for w, _ in widths],
        out_shape=[jax.ShapeDtypeStruct((T, w), dt) for w, dt in widths],
        compiler_params=_cparams(("parallel",)),
        name="inproj_v" if emit_v else "inproj",
    )(x2d, sc, sh, g1, *wts, *tabs, *gains, wm, gb)


def _dsa_kernel(qi_ref, wi_ref, q_ref, ki_ref, k_ref, vt_ref, o_ref,
                keys_ref, bias_ref, meff_ref, acc_ref, kmax_ref, l_ref,
                *, ksel, KT, vis_step, vis_base, lim_lo, lim_hi, q_valid):
    j = pl.program_id(1)
    L = ki_ref.shape[0]
    ntiles = jnp.minimum((j * vis_step + vis_base + KT - 1) // KT, L // KT)
    lane = lax.broadcasted_iota(jnp.int32, (1, QBLK), 1)
    limit = j * vis_step + jnp.where(lane < CHUNK, lim_lo, lim_hi)
    if q_valid < QBLK:
        limit = jnp.where(lane < q_valid, limit, 0)
    row = lax.broadcasted_iota(jnp.int32, (KT, QBLK), 0)

    def tile_off(t):
        return pl.multiple_of(t * KT, KT)

    wi = wi_ref[...] * (IDX_DIM ** -0.5)
    qit = qi_ref[...].astype(jnp.float32).T.astype(jnp.bfloat16)
    qit = jnp.concatenate([qit[h * IDX_DIM:(h + 1) * IDX_DIM, :] for h in range(N_IDX_HEADS)], axis=1)
    qit = jnp.concatenate([qit, jnp.zeros((LANES - IDX_DIM, N_IDX_HEADS * QBLK), jnp.bfloat16)], axis=0)

    def idx_body(t, carry):
        off = tile_off(t)
        s = _bdot(ki_ref[pl.ds(off, KT), :], qit)
        acc = jnp.zeros((KT, QBLK), jnp.float32)
        for h in range(N_IDX_HEADS):
            acc = acc + jnp.maximum(s[:, h * QBLK:(h + 1) * QBLK], 0.0) * wi[h:h + 1, :]
        bits = pltpu.bitcast(acc, jnp.int32)
        key = jnp.where(bits >= 0, bits, bits ^ 0x7FFFFFFF)
        keys_ref[pl.ds(off, KT), :] = jnp.where(row + off < limit, key, INT_MIN)
        return carry

    lax.fori_loop(0, ntiles, idx_body, 0)

    KC = min(KT, KT_COUNT)
    nctiles = jnp.minimum((j * vis_step + vis_base + KC - 1) // KC, L // KC)
    rowc = lax.broadcasted_iota(jnp.int32, (KC, QBLK), 0)

    def count(indicator):
        def body(t, acc):
            off = pl.multiple_of(t * KC, KC)
            ones = indicator(keys_ref[pl.ds(off, KC), :], off)
            return acc + ones.reshape(KC // SUBLANES, SUBLANES, QBLK).sum(axis=0)
        acc = lax.fori_loop(0, nctiles, body, jnp.zeros((SUBLANES, QBLK), jnp.int32))
        return acc.sum(axis=0, keepdims=True)

    def bit_body(i, tu):
        cand_u = tu | lax.shift_left(jnp.int32(1), 31 - i)
        cand_s = cand_u ^ INT_MIN
        cnt = count(lambda key, off: jnp.where(key >= cand_s, 1, 0))
        return jnp.where(cnt >= ksel, cand_u, tu)

    tu = lax.fori_loop(0, 32, bit_body, jnp.zeros((1, QBLK), jnp.int32))
    thr = jnp.maximum(tu ^ INT_MIN, INT_MIN + 1)

    need = count(lambda key, off: jnp.where(key >= thr, 1, 0)) > ksel
    meff_ref[...] = jnp.full(meff_ref.shape, IDX_BIG, jnp.int32)

    @pl.when(jnp.max(need.astype(jnp.int32)) > 0)
    def _():
        want = ksel - count(lambda key, off: jnp.where(key > thr, 1, 0))
        nbits = int(L).bit_length()

        def m_body(i, mp):
            cand = mp | lax.shift_left(jnp.int32(1), nbits - 1 - i)
            f = count(lambda key, off: jnp.where(key == thr, jnp.where(rowc + off < cand, 1, 0), 0))
            return jnp.where(f < want, cand, mp)

        mp = lax.fori_loop(0, nbits, m_body, jnp.zeros((1, QBLK), jnp.int32))
        meff_ref[...] = jnp.broadcast_to(jnp.where(need, mp, IDX_BIG), meff_ref.shape)

    meff = meff_ref[0:1, :]

    def bias_body(t, carry):
        off = tile_off(t)
        key = keys_ref[pl.ds(off, KT), :]
        tie = jnp.where(row + off <= meff, 0.0, NEG_BIG)
        bias_ref[pl.ds(off, KT), :] = jnp.where(key == thr, tie, jnp.where(key > thr, 0.0, NEG_BIG))
        return carry

    lax.fori_loop(0, ntiles, bias_body, 0)

    @pl.when(j == 0)
    def _():
        ones_c = jnp.ones((HEAD_DIM, LANES), jnp.bfloat16)

        def kn_body(t, carry):
            off = tile_off(t)
            out = []
            for g in range(N_KV_HEADS):
                kf = k_ref[pl.ds(off, KT), g * HEAD_DIM:(g + 1) * HEAD_DIM].astype(jnp.float32)
                n2 = _bdot((kf * kf).astype(jnp.bfloat16), ones_c)
                out.append(jnp.maximum(carry[g], jnp.max(n2, axis=0, keepdims=True)))
            return tuple(out)

        kn = lax.fori_loop(0, L // KT, kn_body, (jnp.zeros((1, LANES), jnp.float32),) * N_KV_HEADS)
        for g in range(N_KV_HEADS):
            kmax_ref[g:g + 1, :] = kn[g]

    qgs = [jnp.concatenate([q_ref[:, (g * Q_PER_KV + h) * HEAD_DIM:(g * Q_PER_KV + h + 1) * HEAD_DIM]
                            for h in range(Q_PER_KV)], axis=0) for g in range(N_KV_HEADS)]
    acc_ref[...] = jnp.zeros(acc_ref.shape, jnp.float32)
    W = Q_PER_KV * QBLK

    def att_body(t, carry):
        off = tile_off(t)
        b = bias_ref[pl.ds(off, KT), :]
        b4 = jnp.concatenate([b] * Q_PER_KV, axis=1)
        out = []
        for g in range(N_KV_HEADS):
            m_old, l_old = carry[2 * g], carry[2 * g + 1]
            kt = k_ref[pl.ds(off, KT), g * HEAD_DIM:(g + 1) * HEAD_DIM]
            s = _dot_nt(kt, qgs[g]) + b4
            m_new = jnp.maximum(m_old, jnp.max(s, axis=0, keepdims=True))
            alpha = jnp.exp2(m_old - m_new)
            p = jnp.exp2(s - m_new)
            l_new = alpha * l_old + jnp.sum(p, axis=0, keepdims=True)
            vt = vt_ref[g * HEAD_DIM:(g + 1) * HEAD_DIM, pl.ds(off, KT)]
            acc_ref[g] = acc_ref[g] * alpha + _bdot(vt, p.astype(jnp.bfloat16))
            out += [m_new, l_new]
        return tuple(out)

    ones_l = jnp.ones((SUBLANES, HEAD_DIM), jnp.bfloat16)
    ceil = []
    for g in range(N_KV_HEADS):
        qf = qgs[g].astype(jnp.float32)
        qn2 = _dot_nt(ones_l, (qf * qf).astype(jnp.bfloat16))[0:1, :]
        kn2 = jnp.concatenate([kmax_ref[g:g + 1, :]] * Q_PER_KV, axis=1)
        ceil.append(jnp.sqrt(qn2 * kn2) * 1.02)
    small = jnp.max(jnp.maximum(ceil[0], ceil[1])) <= SOFTMAX_CEIL_MAX

    def fixed_body(t, carry):
        off = tile_off(t)
        b = bias_ref[pl.ds(off, KT), :]
        b4 = jnp.concatenate([b] * Q_PER_KV, axis=1)
        out = []
        for g in range(N_KV_HEADS):
            kt = k_ref[pl.ds(off, KT), g * HEAD_DIM:(g + 1) * HEAD_DIM]
            p = jnp.exp2(_dot_nt(kt, qgs[g]) + (b4 - ceil[g]))
            vt = vt_ref[g * HEAD_DIM:(g + 1) * HEAD_DIM, pl.ds(off, KT)]
            acc_ref[g] = acc_ref[g] + _bdot(vt, p.astype(jnp.bfloat16))
            out.append(carry[g] + jnp.sum(p, axis=0, keepdims=True))
        return tuple(out)

    @pl.when(small)
    def _():
        sums = lax.fori_loop(0, ntiles, fixed_body, (jnp.zeros((1, W), jnp.float32),) * N_KV_HEADS)
        for g in range(N_KV_HEADS):
            l_ref[g, 0:1, :] = sums[g]

    @pl.when(jnp.logical_not(small))
    def _():
        init = (jnp.full((1, W), NEG_BIG, jnp.float32), jnp.zeros((1, W), jnp.float32)) * N_KV_HEADS
        fin = lax.fori_loop(0, ntiles, att_body, init)
        for g in range(N_KV_HEADS):
            l_ref[g, 0:1, :] = fin[2 * g + 1]

    for g in range(N_KV_HEADS):
        l = l_ref[g, 0:1, :]
        o = acc_ref[g] * (1.0 / jnp.where(l > 0.0, l, 1.0))
        for h in range(Q_PER_KV):
            hd = g * Q_PER_KV + h
            o_ref[:, hd * HEAD_DIM:(hd + 1) * HEAD_DIM] = o[:, h * QBLK:(h + 1) * QBLK].T.astype(jnp.bfloat16)


def _dsa(qi, wit, q, ki16, k16, vt, *, ksel, kt, vis_step, vis_base, lim_lo, lim_hi, q_valid=QBLK):
    NB, L, _ = ki16.shape
    Sq = wit.shape[2]
    nq = Sq // QBLK
    assert L % kt == 0
    kern = functools.partial(_dsa_kernel, ksel=ksel, KT=kt, vis_step=vis_step, vis_base=vis_base,
                             lim_lo=lim_lo, lim_hi=lim_hi, q_valid=q_valid)
    return pl.pallas_call(
        kern,
        grid=(NB, nq),
        in_specs=[pl.BlockSpec((QBLK, qi.shape[1]), lambda b, j: (b * nq + j, 0)),
                  pl.BlockSpec((None, N_IDX_HEADS, QBLK), lambda b, j: (b, 0, j)),
                  pl.BlockSpec((QBLK, q.shape[1]), lambda b, j: (b * nq + j, 0)),
                  pl.BlockSpec((None, L, LANES), lambda b, j: (b, 0, 0)),
                  pl.BlockSpec((None, L, k16.shape[2]), lambda b, j: (b, 0, 0)),
                  pl.BlockSpec((None, vt.shape[1], L), lambda b, j: (b, 0, 0))],
        out_specs=pl.BlockSpec((QBLK, q.shape[1]), lambda b, j: (b * nq + j, 0)),
        out_shape=jax.ShapeDtypeStruct(q.shape, jnp.bfloat16),
        scratch_shapes=[pltpu.VMEM((L, QBLK), jnp.int32),
                        pltpu.VMEM((L, QBLK), jnp.float32),
                        pltpu.VMEM((SUBLANES, QBLK), jnp.int32),
                        pltpu.VMEM((N_KV_HEADS, HEAD_DIM, Q_PER_KV * QBLK), jnp.float32),
                        pltpu.VMEM((SUBLANES, LANES), jnp.float32),
                        pltpu.VMEM((N_KV_HEADS, SUBLANES, Q_PER_KV * QBLK), jnp.float32)],
        compiler_params=_cparams(("parallel", "arbitrary")),
        name="dsa",
    )(qi, wit, q, ki16, k16, vt)


def _merge_kernel(a_ref, b_ref, sg_ref, x_ref, gt_ref, sc2_ref, sh2_ref, g2_ref,
                  wa_ref, wb_ref, wo_ref, wq_ref, xm_ref, h2_ref, qp_ref):
    D = x_ref.shape[1]
    sg = sg_ref[...].astype(jnp.float32)
    merged = sg[:, :D] * _bdot(a_ref[...], wa_ref[...]) + sg[:, D:] * _bdot(b_ref[...], wb_ref[...])
    y = _bdot(merged.astype(jnp.bfloat16), wo_ref[...])
    xm = x_ref[...] + gt_ref[...] * y
    xm_ref[...] = xm
    h2 = _modulate(xm, g2_ref[...], sc2_ref[...], sh2_ref[...])
    h2_ref[...] = h2
    qp_ref[...] = _bdot(h2.astype(jnp.bfloat16), wq_ref[...]).astype(jnp.bfloat16)


def _merge(a, b, sg, x2d, gt, sc2, sh2, g2, wa, wb, wo, wq, *, tm):
    T, D = x2d.shape
    nt = T // tm
    n_mod, R, _ = gt.shape
    tiles_per_mod = nt // n_mod
    const2 = lambda i: (0, 0)
    row = lambda w: pl.BlockSpec((tm, w), lambda i: (i, 0))
    mod_spec = pl.BlockSpec((None, R, D), lambda i: (i // tiles_per_mod, 0, 0))
    nq = wq.shape[1]
    return pl.pallas_call(
        _merge_kernel,
        grid=(nt,),
        in_specs=[row(a.shape[1]), row(b.shape[1]), row(sg.shape[1]), row(D), mod_spec, mod_spec, mod_spec,
                  pl.BlockSpec((1, D), const2)] + [pl.BlockSpec(w.shape, const2) for w in (wa, wb, wo, wq)],
        out_specs=[row(D), row(D), row(nq)],
        out_shape=[jax.ShapeDtypeStruct((T, D), jnp.float32), jax.ShapeDtypeStruct((T, D), jnp.float32),
                   jax.ShapeDtypeStruct((T, nq), jnp.bfloat16)],
        compiler_params=_cparams(("parallel",)),
        name="merge",
    )(a, b, sg, x2d, gt, sc2, sh2, g2, wa, wb, wo, wq)


_PAIR_COUNTS = [PEER_TOPK // (a + 1) for a in range(PEER_TOPK)]
_PAIR_OFFS = [int(v) for v in np.cumsum([0] + _PAIR_COUNTS[:-1])]
_N_PAIRS = sum(_PAIR_COUNTS)
_PAIR_ROWS = -(-_N_PAIRS // SUBLANES) * SUBLANES


def _route_kernel(qp_ref, sk_ref, e_ref, g_ref, s_ref, sv_ref, si_ref, cand_ref, pay_ref, gv_ref):
    TT = qp_ref.shape[0]
    half = PEER_KEY_DIM // 2
    NC = PEER_HEADS * 2
    NEG_INF = -jnp.inf

    for c in range(NC):
        s_ref[c] = _dot_nt(sk_ref[c // 2, c % 2], qp_ref[:, c * half:(c + 1) * half])

    rowk = lax.broadcasted_iota(jnp.int32, (PEER_N_KEYS, TT), 0)

    def key_round(r, carry):
        for c in range(NC):
            s = s_ref[c]
            m = jnp.max(s, axis=0, keepdims=True)
            am = jnp.min(jnp.where(s == m, rowk, PEER_N_KEYS), axis=0, keepdims=True)
            sv_ref[c, pl.ds(r, 1), :] = m
            si_ref[c, pl.ds(r, 1), :] = am
            s_ref[c] = jnp.where(rowk == am, NEG_INF, s)
        return carry

    lax.fori_loop(0, PEER_TOPK, key_round, 0)

    for h in range(PEER_HEADS):
        sv0, sv1 = sv_ref[2 * h], sv_ref[2 * h + 1]
        si0, si1 = si_ref[2 * h], si_ref[2 * h + 1]
        cand_ref[h, _PAIR_ROWS - SUBLANES:_PAIR_ROWS, :] = jnp.full((SUBLANES, TT), NEG_INF, jnp.float32)
        pay_ref[h, _PAIR_ROWS - SUBLANES:_PAIR_ROWS, :] = jnp.zeros((SUBLANES, TT), jnp.int32)
        for a in range(PEER_TOPK):
            n, off = _PAIR_COUNTS[a], _PAIR_OFFS[a]
            cand_ref[h, off:off + n, :] = sv0[a:a + 1, :] + sv1[0:n, :]
            pay_ref[h, off:off + n, :] = si0[a:a + 1, :] * PEER_N_KEYS + si1[0:n, :]

    rowp = lax.broadcasted_iota(jnp.int32, (_PAIR_ROWS, TT), 0)

    def pair_round(r, carry):
        for h in range(PEER_HEADS):
            s = cand_ref[h]
            m = jnp.max(s, axis=0, keepdims=True)
            am = jnp.min(jnp.where(s == m, rowp, _PAIR_ROWS), axis=0, keepdims=True)
            hit = rowp == am
            gv_ref[h, pl.ds(r, 1), :] = m
            e_ref[pl.ds(h * PEER_TOPK + r, 1), :] = jnp.sum(jnp.where(hit, pay_ref[h], 0), axis=0, keepdims=True)
            cand_ref[h] = jnp.where(hit, NEG_INF, s)
        return carry

    lax.fori_loop(0, PEER_TOPK, pair_round, 0)

    for h in range(PEER_HEADS):
        gv = gv_ref[h]
        ex = jnp.exp(gv - jnp.max(gv, axis=0, keepdims=True))
        g_ref[h * PEER_TOPK:(h + 1) * PEER_TOPK, :] = ex / jnp.sum(ex, axis=0, keepdims=True)


def _route(qp, sk16, *, tt):
    T = qp.shape[0]
    nt = T // tt
    ne = PEER_HEADS * PEER_TOPK
    out = jax.ShapeDtypeStruct((nt, ne, tt), jnp.int32)
    blk = pl.BlockSpec((None, ne, tt), lambda i: (i, 0, 0))
    return pl.pallas_call(
        _route_kernel,
        grid=(nt,),
        in_specs=[pl.BlockSpec((tt, qp.shape[1]), lambda i: (i, 0)),
                  pl.BlockSpec(sk16.shape, lambda i: (0, 0, 0, 0))],
        out_specs=[blk, blk],
        out_shape=[out, jax.ShapeDtypeStruct(out.shape, jnp.float32)],
        scratch_shapes=[pltpu.VMEM((PEER_HEADS * 2, PEER_N_KEYS, tt), jnp.float32),
                        pltpu.VMEM((PEER_HEADS * 2, PEER_TOPK, tt), jnp.float32),
                        pltpu.VMEM((PEER_HEADS * 2, PEER_TOPK, tt), jnp.int32),
                        pltpu.VMEM((PEER_HEADS, _PAIR_ROWS, tt), jnp.float32),
                        pltpu.VMEM((PEER_HEADS, _PAIR_ROWS, tt), jnp.int32),
                        pltpu.VMEM((PEER_HEADS, PEER_TOPK, tt), jnp.float32)],
        compiler_params=_cparams(("parallel",)),
        name="peer_route",
    )(qp, sk16)


def _peer_kernel(ids_ref, g_ref, h2_ref, xm_ref, gt_ref, w_hbm, o_ref, *scratch):
    TP, D = h2_ref.shape
    NE = g_ref.shape[1]
    HS = D // 2 // LANES
    NG = TP // PEER_GROUP
    bf = jnp.bfloat16
    wbufs, sem = scratch[:-1], scratch[-1]

    def issue(grp, half):
        for p in range(PEER_GROUP):
            base = (grp * PEER_GROUP + p) * NE
            slot = half * PEER_GROUP + p
            for k in range(NE):
                src = pl.multiple_of(ids_ref[base + k], PEER_ROWS)
                pltpu.make_async_copy(w_hbm.at[pl.ds(src, PEER_ROWS)],
                                      wbufs[slot].at[pl.ds(k * PEER_ROWS, PEER_ROWS)],
                                      sem.at[slot]).start(priority=k % 2)

    def wait(slot):
        pltpu.make_async_copy(w_hbm.at[pl.ds(0, NE * PEER_ROWS)], wbufs[slot], sem.at[slot]).wait()

    def table(slot, r0):
        lo, hi = [], []
        for s in range(HS):
            w = wbufs[slot][pl.ds(r0 + s, NE, stride=PEER_ROWS), :]
            lo.append(pltpu.bitcast(w << 16, jnp.float32).astype(bf))
            hi.append(pltpu.bitcast(w & jnp.int32(-65536), jnp.float32).astype(bf))
        return jnp.concatenate(lo + hi, axis=1)

    for g0 in range(PEER_SETS - 1):
        issue(g0, g0)

    def ring(i, carry):
        for r in range(PEER_SETS):
            group(PEER_SETS * i + r, r, True)
        return carry

    def group(grp, half, issue_ahead):
        slots = [half * PEER_GROUP + p for p in range(PEER_GROUP)]
        toks = [grp * PEER_GROUP + p for p in range(PEER_GROUP)]
        for slot in slots:
            wait(slot)
        if issue_ahead:
            issue(grp + PEER_SETS - 1, (half + PEER_SETS - 1) % PEER_SETS)
        acts = []
        for t, slot in zip(toks, slots):
            x = jnp.broadcast_to(h2_ref[pl.ds(t, 1), :], (SUBLANES, D)).astype(bf)
            acts.append(_dot_nt(x, table(slot, 0))[0:1, :])
        ys = []
        for t, slot, act in zip(toks, slots, acts):
            c = g_ref[pl.ds(t, 1), :] * jax.nn.gelu(act)
            c8 = jnp.broadcast_to(c, (SUBLANES, NE)).astype(bf)
            ys.append(_bdot(c8, table(slot, HS))[0:1, :])
        for t, y in zip(toks, ys):
            gt = gt_ref[pl.ds(t, 1), :] if gt_ref.shape[0] == TP else gt_ref[...]
            o_ref[pl.ds(t, 1), :] = xm_ref[pl.ds(t, 1), :] + gt * y

    n_ring = (NG - (PEER_SETS - 1)) // PEER_SETS
    lax.fori_loop(0, n_ring, ring, 0)
    for grp in range(n_ring * PEER_SETS, NG):
        group(grp, grp % PEER_SETS, grp + PEER_SETS - 1 < NG)


def _peer(ids, g, h2, xm, gt, wtab, *, tp):
    T, D = h2.shape
    NE = ids.shape[1]
    nt = T // tp
    n_mod, R, _ = gt.shape
    tiles_per_mod = nt // n_mod
    row = lambda w: pl.BlockSpec((tp, w), lambda i: (i, 0))
    return pl.pallas_call(
        _peer_kernel,
        grid=(nt,),
        in_specs=[pl.BlockSpec((tp * NE,), lambda i: (i,), memory_space=pltpu.SMEM),
                  row(NE), row(D), row(D),
                  pl.BlockSpec((None, R, D), lambda i: (i // tiles_per_mod, 0, 0)),
                  pl.BlockSpec(memory_space=pl.ANY)],
        out_specs=row(D),
        out_shape=jax.ShapeDtypeStruct((T, D), jnp.float32),
        scratch_shapes=[pltpu.VMEM((NE * PEER_ROWS, LANES), jnp.int32)] * (PEER_SETS * PEER_GROUP)
        + [pltpu.SemaphoreType.DMA((PEER_SETS * PEER_GROUP,))],
        compiler_params=_cparams(("arbitrary",)),
        name="peer_gather",
    )(ids.reshape(-1), g, h2, xm, gt, wtab)


def _pack_bf16_pairs(t):
    E, D = t.shape
    b = lax.bitcast_convert_type(t.astype(jnp.bfloat16), jnp.uint16).astype(jnp.uint32)
    w = b[:, :D // 2] | (b[:, D // 2:] << 16)
    return lax.bitcast_convert_type(w, jnp.int32).reshape(E, D // 2 // LANES, LANES)


def _rope_tables(pos, rot_dim, width):
    half = rot_dim // 2
    inv_freq = ROPE_THETA ** (-jnp.arange(half, dtype=jnp.float32) / half)
    ang = pos.astype(jnp.float32)[..., None] * inv_freq
    cos, sin = jnp.cos(ang), jnp.sin(ang)
    w = np.arange(LANES) % width
    first, second = w < half, (w >= half) & (w < rot_dim)
    src = np.where(first, w, np.where(second, w - half, 0))
    c = jnp.where(first | second, cos[:, src], 1.0)
    sa = jnp.where(first, -sin[:, src], 0.0)
    sb = jnp.where(second, sin[:, src], 0.0)
    return c, sa, sb


def _gmlp_mask():
    i = np.arange(GMLP_CHUNK)
    return (i[None, :] // CHUNK) <= (i[:, None] // CHUNK)


def _rows_per_seq(v, reps, tm):
    nb, d = v.shape
    return jnp.repeat(v, reps, axis=0).reshape(nb * reps // tm, tm, d)


def _layer(xp, xs, cache_k, cache_v, cache_kidx, c_prompt, c_sample, ada_w, ada_b, norm1_g, norm2_g, w_in,
           q_norm_g, k_norm_g, kidx_norm_g, gmlp_v_norm_g, gmlp_ws, gmlp_b, w_branch_a, w_branch_b, w_out,
           peer_wq, peer_subkeys, peer_u, peer_v):
    B, S, D = xp.shape
    DB, DS, _ = xs.shape
    P = cache_k.shape[1]
    Tp, Ts = B * S, DB * DS
    tm_p, tm_s = 2 * GMLP_CHUNK, GMLP_CHUNK
    assert S % tm_p == 0 and GMLP_CHUNK % DS == 0 and Ts % tm_s == 0 and DS <= CHUNK and P % CHUNK == 0
    tp_p, tp_s = min(PEER_TOK, S), min(PEER_TOK, Ts)
    assert S % tp_p == 0 and Ts % tp_s == 0
    assert tp_p % PEER_GROUP == 0 and tp_s % PEER_GROUP == 0 and min(tp_p, tp_s) // PEER_GROUP >= PEER_SETS
    bf = jnp.bfloat16
    f32 = jnp.float32

    nrow = B + DB
    npad = -nrow % SUBLANES
    c_all = jnp.concatenate([c_prompt, c_sample, jnp.zeros((npad, D), f32)], axis=0)
    mod = _adaln(c_all, ada_w, ada_b)
    mods = jnp.split(mod, 6, axis=-1)
    mp = [m[:B].reshape(B, 1, D) for m in mods]
    ms = [_rows_per_seq(m[B:B + DB], DS, tm_s) for m in mods]

    sizes = (GMLP_WIDTH, GMLP_WIDTH, N_HEADS * HEAD_DIM, N_KV_HEADS * HEAD_DIM, N_KV_HEADS * HEAD_DIM,
             N_IDX_HEADS * IDX_DIM, IDX_DIM, N_IDX_HEADS, 2 * D)
    pts = [int(s) for s in np.cumsum(sizes)[:-1]]
    wu, wv, wq, wk, wva, wqi, wki, wwi, wg = jnp.split(w_in.astype(bf), pts, axis=-1)
    wkw = jnp.concatenate([wki, wwi, jnp.zeros((D, LANES - IDX_DIM - N_IDX_HEADS), bf)], axis=-1)
    wts = (wu, wv, wq, wk, wva, wqi, wg, wkw)
    kig = jnp.concatenate([kidx_norm_g, jnp.zeros((LANES - IDX_DIM,), f32)]).reshape(1, LANES)
    gains = (q_norm_g.reshape(1, HEAD_DIM), k_norm_g.reshape(1, HEAD_DIM), kig, gmlp_v_norm_g.reshape(1, GMLP_WIDTH))
    g1 = norm1_g.reshape(1, D)
    g2 = norm2_g.reshape(1, D)

    wmask = jnp.where(_gmlp_mask()[None], gmlp_ws, 0.0)
    wm_p = wmask.astype(bf)
    gb_p = jnp.repeat(gmlp_b.T, GMLP_GROUP_DIM, axis=1)
    reps = GMLP_CHUNK // DS
    eye = jnp.eye(reps, dtype=f32)
    wm_s = jnp.einsum("ab,gij->gaibj", eye, wmask[:, :DS, :DS]).reshape(GMLP_GROUPS, GMLP_CHUNK, GMLP_CHUNK).astype(bf)
    gb_s = jnp.tile(gb_p[:DS], (reps, 1))

    pos_p = jnp.arange(S)
    pos_s = P + (jnp.arange(tm_s) % DS)
    tabs_p = _rope_tables(pos_p, ROT_DIM, HEAD_DIM) + _rope_tables(pos_p, IDX_ROT_DIM, IDX_DIM)
    tabs_s = _rope_tables(pos_s, ROT_DIM, HEAD_DIM) + _rope_tables(pos_s, IDX_ROT_DIM, IDX_DIM)

    x2p = xp.reshape(Tp, D)
    x2s = xs.reshape(Ts, D)

    (a_p, q_p, k32_p, k16_p, va32_p, va16_p, qi_p, kw32_p, ki16_p, sg_p) = _inproj(
        x2p, mp[1], mp[0], g1, wts, tabs_p, gains, wm_p, gb_p, tm=tm_p, tab_blocks=S // tm_p, emit_v=False)
    wit_p = kw32_p[:, IDX_DIM:IDX_DIM + N_IDX_HEADS].reshape(B, S, N_IDX_HEADS).transpose(0, 2, 1)
    vt_p = va16_p.reshape(B, S, -1).transpose(0, 2, 1)
    b_p = _dsa(qi_p, wit_p, q_p, ki16_p.reshape(B, S, LANES), k16_p.reshape(B, S, -1), vt_p,
               ksel=min(TOPK_MAX, S // 4), kt=min(KT_PROMPT, S), vis_step=QBLK, vis_base=QBLK,
               lim_lo=CHUNK, lim_hi=2 * CHUNK)

    (a_s, v_s, q_s, k32_s, k16_s, va32_s, va16_s, qi_s, kw32_s, ki16_s, sg_s) = _inproj(
        x2s, ms[1], ms[0], g1, wts, tabs_s, gains, wm_s, gb_s, tm=tm_s, tab_blocks=1, emit_v=True)
    Lk = P + DS
    Lpad = -(-Lk // KT_SAMPLE) * KT_SAMPLE

    def pad_q(a):
        w = a.shape[-1]
        return jnp.pad(a.reshape(DB, DS, w), ((0, 0), (0, QBLK - DS), (0, 0))).reshape(DB * QBLK, w)

    def cat_keys(cache, new, w):
        return jnp.concatenate([cache.reshape(DB, P, -1).astype(bf), new.reshape(DB, DS, -1),
                                jnp.zeros((DB, Lpad - Lk, w), bf)], axis=1)

    wit_s = jnp.pad(kw32_s[:, IDX_DIM:IDX_DIM + N_IDX_HEADS].reshape(DB, DS, N_IDX_HEADS).transpose(0, 2, 1),
                    ((0, 0), (0, 0), (0, QBLK - DS)))
    kidx_cache = jnp.pad(cache_kidx, ((0, 0), (0, 0), (0, LANES - IDX_DIM)))
    ki_all = cat_keys(kidx_cache, ki16_s, LANES)
    k_all = cat_keys(cache_k, k16_s, N_KV_HEADS * HEAD_DIM)
    vt_all = cat_keys(cache_v, va16_s, N_KV_HEADS * HEAD_DIM).transpose(0, 2, 1)
    b_s = _dsa(pad_q(qi_s), wit_s, pad_q(q_s), ki_all, k_all, vt_all,
               ksel=min(TOPK_MAX, Lk // 4), kt=KT_SAMPLE, vis_step=0, vis_base=Lpad, lim_lo=Lk, lim_hi=Lk,
               q_valid=DS)
    b_s = b_s.reshape(DB, QBLK, -1)[:, :DS].reshape(Ts, -1)

    wa, wb, wo, wpq = w_branch_a.astype(bf), w_branch_b.astype(bf), w_out.astype(bf), peer_wq.astype(bf)
    sk16 = peer_subkeys.astype(bf)
    wtab = jnp.concatenate([_pack_bf16_pairs(peer_u), _pack_bf16_pairs(peer_v)], axis=1).reshape(-1, LANES)

    def tail(a, b, sg, x2d, m, tm):
        xm, h2, qp = _merge(a, b, sg, x2d, m[2], m[4], m[3], g2, wa, wb, wo, wpq, tm=tm)
        e, g = _route(qp, sk16, tt=LANES)
        T = x2d.shape[0]
        ids = e.transpose(0, 2, 1).reshape(T, PEER_HEADS * PEER_TOPK) * PEER_ROWS
        gw = g.transpose(0, 2, 1).reshape(T, PEER_HEADS * PEER_TOPK)
        return ids, gw, h2, xm

    ids_p, gw_p, h2_p, xm_p = tail(a_p, b_p, sg_p, x2p, mp, tm_p)
    out_p = _peer(ids_p, gw_p, h2_p, xm_p, mp[5], wtab, tp=tp_p)
    ms_peer = [_rows_per_seq(m[B:B + DB], DS, tp_s) for m in mods]
    ids_s, gw_s, h2_s, xm_s = tail(a_s, b_s, sg_s, x2s, ms, tm_s)
    out_s = _peer(ids_s, gw_s, h2_s, xm_s, ms_peer[5], wtab, tp=tp_s)

    new = (k32_p.reshape(B, S, N_KV_HEADS, HEAD_DIM), va32_p.reshape(B, S, N_KV_HEADS, HEAD_DIM),
           kw32_p[:, :IDX_DIM].reshape(B, S, IDX_DIM),
           k32_s.reshape(DB, DS, N_KV_HEADS, HEAD_DIM), va32_s.reshape(DB, DS, N_KV_HEADS, HEAD_DIM),
           kw32_s[:, :IDX_DIM].reshape(DB, DS, IDX_DIM), v_s.reshape(DB, DS, GMLP_WIDTH))
    return out_p.reshape(B, S, D), out_s.reshape(DB, DS, D), new


def kernel(x_prompt, x_sample, cache_k, cache_v, cache_kidx, c_prompt, c_sample, ada_w, ada_b, norm1_g, norm2_g,
           w_in, q_norm_g, k_norm_g, kidx_norm_g, gmlp_v_norm_g, gmlp_ws, gmlp_b, w_branch_a, w_branch_b, w_out,
           peer_wq, peer_subkeys, peer_u, peer_v):
    xp, xs = x_prompt, x_sample
    per_layer = []
    for l in range(ada_w.shape[0]):
        xp, xs, new = _layer(xp, xs, cache_k[l], cache_v[l], cache_kidx[l], c_prompt, c_sample, ada_w[l], ada_b[l],
                             norm1_g[l], norm2_g[l], w_in[l], q_norm_g[l], k_norm_g[l], kidx_norm_g[l],
                             gmlp_v_norm_g[l], gmlp_ws[l], gmlp_b[l], w_branch_a[l], w_branch_b[l], w_out[l],
                             peer_wq[l], peer_subkeys[l], peer_u[l], peer_v[l])
        per_layer.append(new)
    stacked = tuple(jnp.stack([n[i] for n in per_layer]) for i in range(7))
    return (xp, xs) + stacked
```

```python
import functools

import jax
import jax.numpy as jnp
import numpy as np
from jax import lax
from jax.experimental import pallas as pl
from jax.experimental.pallas import tpu as pltpu

CHUNK = 64
EPS = 1e-6
ROPE_THETA = 500000.0
GMLP_CHUNK = 128
GMLP_GROUPS = 8
GMLP_GROUP_DIM = 128
GMLP_WIDTH = GMLP_GROUPS * GMLP_GROUP_DIM
N_HEADS = 8
N_KV_HEADS = 2
Q_PER_KV = N_HEADS // N_KV_HEADS
HEAD_DIM = 128
ROT_DIM = HEAD_DIM // 4
N_IDX_HEADS = 8
IDX_DIM = 64
IDX_ROT_DIM = IDX_DIM // 4
TOPK_MAX = 256
PEER_HEADS = 8
PEER_N_KEYS = 128
PEER_KEY_DIM = 256
PEER_TOPK = 16

LANES = 128
SUBLANES = 8
VMEM_LIMIT_BYTES = 56 * 1024 * 1024

QBLK = LANES
KT_PROMPT = 1024
KT_SAMPLE = 256
KT_COUNT = 512
QK_FOLD = HEAD_DIM ** -0.5 * float(np.log2(np.e))
TOPK_EARLY_PASSES = 26
SOFTMAX_CEIL_MAX = 50.0
PEER_TOK = 256
PEER_SETS = 3
PEER_GROUP = 4
PEER_ROWS = 8
INT_MIN = -(2 ** 31)
NEG_BIG = -1e30
IDX_BIG = 2 ** 30


def _cparams(sem):
    return pltpu.CompilerParams(dimension_semantics=sem, vmem_limit_bytes=VMEM_LIMIT_BYTES)


def _rms(x, width):
    return x * lax.rsqrt(jnp.sum(x * x, axis=-1, keepdims=True) * (1.0 / width) + EPS)


def _modulate(x, g, sc, sh):
    return _rms(x, x.shape[-1]) * g * (1.0 + sc) + sh


def _rope(x, c, sa, sb, half):
    n = x.shape[-1]
    return x * c + pltpu.roll(x, n - half, 1) * sa + pltpu.roll(x, half, 1) * sb


def _bdot(a, b):
    return jnp.dot(a, b, preferred_element_type=jnp.float32)


def _dot_nt(a, b):
    return lax.dot_general(a, b, (((1,), (1,)), ((), ())), preferred_element_type=jnp.float32)


def _adaln_kernel(c_ref, w_ref, b_ref, o_ref):
    c = c_ref[...]
    h = (c * jax.nn.sigmoid(c)).astype(jnp.bfloat16)
    o_ref[...] = _bdot(h, w_ref[...].astype(jnp.bfloat16)) + b_ref[...]


def _adaln(c, w, b):
    rows, d = c.shape
    n = w.shape[1]
    tn = n // 4
    return pl.pallas_call(
        _adaln_kernel,
        grid=(n // tn,),
        in_specs=[pl.BlockSpec((rows, d), lambda j: (0, 0)),
                  pl.BlockSpec((d, tn), lambda j: (0, j)),
                  pl.BlockSpec((1, tn), lambda j: (0, j))],
        out_specs=pl.BlockSpec((rows, tn), lambda j: (0, j)),
        out_shape=jax.ShapeDtypeStruct((rows, n), jnp.float32),
        compiler_params=_cparams(("arbitrary",)),
        name="adaln",
    )(c, w, b.reshape(1, n))


def _inproj_kernel(x_ref, sc_ref, sh_ref, g1_ref,
                   wu_ref, wv_ref, wq_ref, wk_ref, wva_ref, wqi_ref, wg_ref, wkw_ref,
                   c128_ref, sa128_ref, sb128_ref, c64_ref, sa64_ref, sb64_ref,
                   qg_ref, kg_ref, kig_ref, gvg_ref, wm_ref, gb_ref,
                   *out_refs, emit_v):
    if emit_v:
        (a_ref, v_ref, q_ref, k32_ref, k16_ref, va32_ref, va16_ref, qi_ref, kw32_ref, ki16_ref, sg_ref) = out_refs
    else:
        (a_ref, q_ref, k32_ref, k16_ref, va32_ref, va16_ref, qi_ref, kw32_ref, ki16_ref, sg_ref) = out_refs
        v_ref = None
    tm = x_ref.shape[0]
    h = _modulate(x_ref[...], g1_ref[...], sc_ref[...], sh_ref[...]).astype(jnp.bfloat16)

    c128, sa128, sb128 = c128_ref[...], sa128_ref[...], sb128_ref[...]
    c64, sa64, sb64 = c64_ref[...], sa64_ref[...], sb64_ref[...]

    v = _rms(jax.nn.gelu(_bdot(h, wv_ref[...])), GMLP_WIDTH) * gvg_ref[...]
    if v_ref is not None:
        v_ref[...] = v
    v16 = v.astype(jnp.bfloat16)
    u = jax.nn.gelu(_bdot(h, wu_ref[...]))
    for c in range(tm // GMLP_CHUNK):
        rows = slice(c * GMLP_CHUNK, (c + 1) * GMLP_CHUNK)
        for g in range(GMLP_GROUPS):
            cols = slice(g * GMLP_GROUP_DIM, (g + 1) * GMLP_GROUP_DIM)
            s = _bdot(wm_ref[g], v16[rows, cols]) + gb_ref[:, cols]
            a_ref[rows, cols] = (u[rows, cols] * s).astype(jnp.bfloat16)

    q = _bdot(h, wq_ref[...])
    for hd in range(N_HEADS):
        cols = slice(hd * HEAD_DIM, (hd + 1) * HEAD_DIM)
        qh = _rope(_rms(q[:, cols], HEAD_DIM) * qg_ref[...], c128, sa128, sb128, ROT_DIM // 2)
        q_ref[:, cols] = (qh * QK_FOLD).astype(jnp.bfloat16)
    k = _bdot(h, wk_ref[...])
    for hd in range(N_KV_HEADS):
        cols = slice(hd * HEAD_DIM, (hd + 1) * HEAD_DIM)
        kh = _rope(_rms(k[:, cols], HEAD_DIM) * kg_ref[...], c128, sa128, sb128, ROT_DIM // 2)
        k32_ref[:, cols] = kh
        k16_ref[:, cols] = kh.astype(jnp.bfloat16)
    va = _bdot(h, wva_ref[...])
    va32_ref[...] = va
    va16_ref[...] = va.astype(jnp.bfloat16)

    qi = _bdot(h, wqi_ref[...])
    for p in range(N_IDX_HEADS * IDX_DIM // LANES):
        cols = slice(p * LANES, (p + 1) * LANES)
        qi_ref[:, cols] = _rope(qi[:, cols], c64, sa64, sb64, IDX_ROT_DIM // 2).astype(jnp.bfloat16)
    kw = _bdot(h, wkw_ref[...])
    is_ki = lax.broadcasted_iota(jnp.int32, kw.shape, 1) < IDX_DIM
    kis = jnp.where(is_ki, kw, 0.0)
    kin = kis * lax.rsqrt(jnp.sum(kis * kis, axis=-1, keepdims=True) * (1.0 / IDX_DIM) + EPS) * kig_ref[...]
    kir = _rope(kin, c64, sa64, sb64, IDX_ROT_DIM // 2)
    kw32_ref[...] = jnp.where(is_ki, kir, kw * (N_IDX_HEADS ** -0.5))
    ki16_ref[...] = jnp.where(is_ki, kir, 0.0).astype(jnp.bfloat16)

    sg_ref[...] = jax.nn.sigmoid(_bdot(h, wg_ref[...])).astype(jnp.bfloat16)


def _inproj(x2d, sc, sh, g1, wts, tabs, gains, wm, gb, *, tm, tab_blocks, emit_v):
    T, D = x2d.shape
    nt = T // tm
    n_mod, R, _ = sc.shape
    tiles_per_mod = nt // n_mod
    const2 = lambda i: (0, 0)
    mod_spec = pl.BlockSpec((None, R, D), lambda i: (i // tiles_per_mod, 0, 0))
    tab_spec = pl.BlockSpec((tm, LANES), lambda i: (i % tab_blocks, 0))
    in_specs = ([pl.BlockSpec((tm, D), lambda i: (i, 0)), mod_spec, mod_spec, pl.BlockSpec((1, D), const2)]
                + [pl.BlockSpec(w.shape, const2) for w in wts]
                + [tab_spec] * 6
                + [pl.BlockSpec(g.shape, const2) for g in gains]
                + [pl.BlockSpec(wm.shape, lambda i: (0, 0, 0)), pl.BlockSpec(gb.shape, const2)])
    widths = [(GMLP_WIDTH, jnp.bfloat16)]
    if emit_v:
        widths.append((GMLP_WIDTH, jnp.float32))
    widths += [(N_HEADS * HEAD_DIM, jnp.bfloat16),
               (N_KV_HEADS * HEAD_DIM, jnp.float32), (N_KV_HEADS * HEAD_DIM, jnp.bfloat16),
               (N_KV_HEADS * HEAD_DIM, jnp.float32), (N_KV_HEADS * HEAD_DIM, jnp.bfloat16),
               (N_IDX_HEADS * IDX_DIM, jnp.bfloat16),
               (LANES, jnp.float32), (LANES, jnp.bfloat16),
               (2 * D, jnp.bfloat16)]
    return pl.pallas_call(
        functools.partial(_inproj_kernel, emit_v=emit_v),
        grid=(nt,),
        in_specs=in_specs,
        out_specs=[pl.BlockSpec((tm, w), lambda i: (i, 0)) for w, _ in widths],
        out_shape=[jax.ShapeDtypeStruct((T, w), dt) for w, dt in widths],
        compiler_params=_cparams(("parallel",)),
        name="inproj_v" if emit_v else "inproj",
    )(x2d, sc, sh, g1, *wts, *tabs, *gains, wm, gb)


def _dsa_kernel(qi_ref, wi_ref, q_ref, ki_ref, k_ref, vt_ref, o_ref,
                keys_ref, bias_ref, meff_ref, acc_ref, kmax_ref, l_ref, tu_ref,
                *, ksel, KT, vis_step, vis_base, lim_lo, lim_hi, q_valid):
    j = pl.program_id(1)
    L = ki_ref.shape[0]
    ntiles = jnp.minimum((j * vis_step + vis_base + KT - 1) // KT, L // KT)
    lane = lax.broadcasted_iota(jnp.int32, (1, QBLK), 1)
    limit = j * vis_step + jnp.where(lane < CHUNK, lim_lo, lim_hi)
    if q_valid < QBLK:
        limit = jnp.where(lane < q_valid, limit, 0)
    row = lax.broadcasted_iota(jnp.int32, (KT, QBLK), 0)

    def tile_off(t):
        return pl.multiple_of(t * KT, KT)

    wi = wi_ref[...] * (IDX_DIM ** -0.5)
    qit = qi_ref[...].astype(jnp.float32).T.astype(jnp.bfloat16)
    qit = jnp.concatenate([qit[h * IDX_DIM:(h + 1) * IDX_DIM, :] for h in range(N_IDX_HEADS)], axis=1)
    qit = jnp.concatenate([qit, jnp.zeros((LANES - IDX_DIM, N_IDX_HEADS * QBLK), jnp.bfloat16)], axis=0)

    def idx_body(t, carry):
        off = tile_off(t)
        s = _bdot(ki_ref[pl.ds(off, KT), :], qit)
        acc = jnp.zeros((KT, QBLK), jnp.float32)
        for h in range(N_IDX_HEADS):
            acc = acc + jnp.maximum(s[:, h * QBLK:(h + 1) * QBLK], 0.0) * wi[h:h + 1, :]
        bits = pltpu.bitcast(acc, jnp.int32)
        key = jnp.where(bits >= 0, bits, bits ^ 0x7FFFFFFF)
        keys_ref[pl.ds(off, KT), :] = jnp.where(row + off < limit, key, INT_MIN)
        return carry

    lax.fori_loop(0, ntiles, idx_body, 0)

    KC = min(KT, KT_COUNT)
    nctiles = jnp.minimum((j * vis_step + vis_base + KC - 1) // KC, L // KC)
    rowc = lax.broadcasted_iota(jnp.int32, (KC, QBLK), 0)

    def count(indicator):
        def body(t, acc):
            off = pl.multiple_of(t * KC, KC)
            ones = indicator(keys_ref[pl.ds(off, KC), :], off)
            return acc + ones.reshape(KC // SUBLANES, SUBLANES, QBLK).sum(axis=0)
        acc = lax.fori_loop(0, nctiles, body, jnp.zeros((SUBLANES, QBLK), jnp.int32))
        return acc.sum(axis=0, keepdims=True)

    def bit_body(i, state):
        tu, settled = state
        cand_u = tu | lax.shift_left(jnp.int32(1), 31 - i)
        cand_s = cand_u ^ INT_MIN
        cnt = count(lambda key, off: jnp.where(key >= cand_s, 1, 0))
        tu = jnp.where(settled > 0, tu, jnp.where(cnt >= ksel, cand_u, tu))
        return tu, jnp.where(cnt == ksel, 1, settled)

    zero_row = jnp.zeros((1, QBLK), jnp.int32)
    tu, settled = lax.fori_loop(0, TOPK_EARLY_PASSES, bit_body, (zero_row, zero_row))
    tu_ref[...] = jnp.broadcast_to(tu, tu_ref.shape)

    @pl.when(jnp.min(settled) == 0)
    def _():
        rest = lax.fori_loop(TOPK_EARLY_PASSES, 32, bit_body, (tu, settled))[0]
        tu_ref[...] = jnp.broadcast_to(rest, tu_ref.shape)

    tu = tu_ref[0:1, :]
    thr = jnp.maximum(tu ^ INT_MIN, INT_MIN + 1)

    need = count(lambda key, off: jnp.where(key >= thr, 1, 0)) > ksel
    meff_ref[...] = jnp.full(meff_ref.shape, IDX_BIG, jnp.int32)

    @pl.when(jnp.max(need.astype(jnp.int32)) > 0)
    def _():
        want = ksel - count(lambda key, off: jnp.where(key > thr, 1, 0))
        nbits = int(L).bit_length()

        def m_body(i, mp):
            cand = mp | lax.shift_left(jnp.int32(1), nbits - 1 - i)
            f = count(lambda key, off: jnp.where(key == thr, jnp.where(rowc + off < cand, 1, 0), 0))
            return jnp.where(f < want, cand, mp)

        mp = lax.fori_loop(0, nbits, m_body, jnp.zeros((1, QBLK), jnp.int32))
        meff_ref[...] = jnp.broadcast_to(jnp.where(need, mp, IDX_BIG), meff_ref.shape)

    meff = meff_ref[0:1, :]

    def bias_body(t, carry):
        off = tile_off(t)
        key = keys_ref[pl.ds(off, KT), :]
        tie = jnp.where(row + off <= meff, 0.0, NEG_BIG)
        bias_ref[pl.ds(off, KT), :] = jnp.where(key == thr, tie, jnp.where(key > thr, 0.0, NEG_BIG))
        return carry

    lax.fori_loop(0, ntiles, bias_body, 0)

    @pl.when(j == 0)
    def _():
        ones_c = jnp.ones((HEAD_DIM, LANES), jnp.bfloat16)

        def kn_body(t, carry):
            off = tile_off(t)
            out = []
            for g in range(N_KV_HEADS):
                kf = k_ref[pl.ds(off, KT), g * HEAD_DIM:(g + 1) * HEAD_DIM].astype(jnp.float32)
                n2 = _bdot((kf * kf).astype(jnp.bfloat16), ones_c)
                out.append(jnp.maximum(carry[g], jnp.max(n2, axis=0, keepdims=True)))
            return tuple(out)

        kn = lax.fori_loop(0, L // KT, kn_body, (jnp.zeros((1, LANES), jnp.float32),) * N_KV_HEADS)
        for g in range(N_KV_HEADS):
            kmax_ref[g:g + 1, :] = kn[g]

    qgs = [jnp.concatenate([q_ref[:, (g * Q_PER_KV + h) * HEAD_DIM:(g * Q_PER_KV + h + 1) * HEAD_DIM]
                            for h in range(Q_PER_KV)], axis=0) for g in range(N_KV_HEADS)]
    acc_ref[...] = jnp.zeros(acc_ref.shape, jnp.float32)
    W = Q_PER_KV * QBLK

    def att_body(t, carry):
        off = tile_off(t)
        b = bias_ref[pl.ds(off, KT), :]
        b4 = jnp.concatenate([b] * Q_PER_KV, axis=1)
        out = []
        for g in range(N_KV_HEADS):
            m_old, l_old = carry[2 * g], carry[2 * g + 1]
            kt = k_ref[pl.ds(off, KT), g * HEAD_DIM:(g + 1) * HEAD_DIM]
            s = _dot_nt(kt, qgs[g]) + b4
            m_new = jnp.maximum(m_old, jnp.max(s, axis=0, keepdims=True))
            alpha = jnp.exp2(m_old - m_new)
            p = jnp.exp2(s - m_new)
            l_new = alpha * l_old + jnp.sum(p, axis=0, keepdims=True)
            vt = vt_ref[g * HEAD_DIM:(g + 1) * HEAD_DIM, pl.ds(off, KT)]
            acc_ref[g] = acc_ref[g] * alpha + _bdot(vt, p.astype(jnp.bfloat16))
            out += [m_new, l_new]
        return tuple(out)

    ones_l = jnp.ones((SUBLANES, HEAD_DIM), jnp.bfloat16)
    ceil = []
    for g in range(N_KV_HEADS):
        qf = qgs[g].astype(jnp.float32)
        qn2 = _dot_nt(ones_l, (qf * qf).astype(jnp.bfloat16))[0:1, :]
        kn2 = jnp.concatenate([kmax_ref[g:g + 1, :]] * Q_PER_KV, axis=1)
        ceil.append(jnp.sqrt(qn2 * kn2) * 1.02)
    small = jnp.max(jnp.maximum(ceil[0], ceil[1])) <= SOFTMAX_CEIL_MAX

    def fixed_body(t, carry):
        off = tile_off(t)
        b = bias_ref[pl.ds(off, KT), :]
        b4 = jnp.concatenate([b] * Q_PER_KV, axis=1)
        out = []
        for g in range(N_KV_HEADS):
            kt = k_ref[pl.ds(off, KT), g * HEAD_DIM:(g + 1) * HEAD_DIM]
            p = jnp.exp2(_dot_nt(kt, qgs[g]) + (b4 - ceil[g]))
            vt = vt_ref[g * HEAD_DIM:(g + 1) * HEAD_DIM, pl.ds(off, KT)]
            acc_ref[g] = acc_ref[g] + _bdot(vt, p.astype(jnp.bfloat16))
            out.append(carry[g] + jnp.sum(p, axis=0, keepdims=True))
        return tuple(out)

    @pl.when(small)
    def _():
        sums = lax.fori_loop(0, ntiles, fixed_body, (jnp.zeros((1, W), jnp.float32),) * N_KV_HEADS)
        for g in range(N_KV_HEADS):
            l_ref[g, 0:1, :] = sums[g]

    @pl.when(jnp.logical_not(small))
    def _():
        init = (jnp.full((1, W), NEG_BIG, jnp.float32), jnp.zeros((1, W), jnp.float32)) * N_KV_HEADS
        fin = lax.fori_loop(0, ntiles, att_body, init)
        for g in range(N_KV_HEADS):
            l_ref[g, 0:1, :] = fin[2 * g + 1]

    for g in range(N_KV_HEADS):
        l = l_ref[g, 0:1, :]
        o = acc_ref[g] * (1.0 / jnp.where(l > 0.0, l, 1.0))
        for h in range(Q_PER_KV):
            hd = g * Q_PER_KV + h
            o_ref[:, hd * HEAD_DIM:(hd + 1) * HEAD_DIM] = o[:, h * QBLK:(h + 1) * QBLK].T.astype(jnp.bfloat16)


def _dsa(qi, wit, q, ki16, k16, vt, *, ksel, kt, vis_step, vis_base, lim_lo, lim_hi, q_valid=QBLK):
    NB, L, _ = ki16.shape
    Sq = wit.shape[2]
    nq = Sq // QBLK
    assert L % kt == 0
    kern = functools.partial(_dsa_kernel, ksel=ksel, KT=kt, vis_step=vis_step, vis_base=vis_base,
                             lim_lo=lim_lo, lim_hi=lim_hi, q_valid=q_valid)
    return pl.pallas_call(
        kern,
        grid=(NB, nq),
        in_specs=[pl.BlockSpec((QBLK, qi.shape[1]), lambda b, j: (b * nq + j, 0)),
                  pl.BlockSpec((None, N_IDX_HEADS, QBLK), lambda b, j: (b, 0, j)),
                  pl.BlockSpec((QBLK, q.shape[1]), lambda b, j: (b * nq + j, 0)),
                  pl.BlockSpec((None, L, LANES), lambda b, j: (b, 0, 0)),
                  pl.BlockSpec((None, L, k16.shape[2]), lambda b, j: (b, 0, 0)),
                  pl.BlockSpec((None, vt.shape[1], L), lambda b, j: (b, 0, 0))],
        out_specs=pl.BlockSpec((QBLK, q.shape[1]), lambda b, j: (b * nq + j, 0)),
        out_shape=jax.ShapeDtypeStruct(q.shape, jnp.bfloat16),
        scratch_shapes=[pltpu.VMEM((L, QBLK), jnp.int32),
                        pltpu.VMEM((L, QBLK), jnp.float32),
                        pltpu.VMEM((SUBLANES, QBLK), jnp.int32),
                        pltpu.VMEM((N_KV_HEADS, HEAD_DIM, Q_PER_KV * QBLK), jnp.float32),
                        pltpu.VMEM((SUBLANES, LANES), jnp.float32),
                        pltpu.VMEM((N_KV_HEADS, SUBLANES, Q_PER_KV * QBLK), jnp.float32),
                        pltpu.VMEM((SUBLANES, QBLK), jnp.int32)],
        compiler_params=_cparams(("parallel", "arbitrary")),
        name="dsa",
    )(qi, wit, q, ki16, k16, vt)


def _merge_kernel(a_ref, b_ref, sg_ref, x_ref, gt_ref, sc2_ref, sh2_ref, g2_ref,
                  wa_ref, wb_ref, wo_ref, wq_ref, xm_ref, h2_ref, qp_ref):
    D = x_ref.shape[1]
    sg = sg_ref[...].astype(jnp.float32)
    merged = sg[:, :D] * _bdot(a_ref[...], wa_ref[...]) + sg[:, D:] * _bdot(b_ref[...], wb_ref[...])
    y = _bdot(merged.astype(jnp.bfloat16), wo_ref[...])
    xm = x_ref[...] + gt_ref[...] * y
    xm_ref[...] = xm
    h2 = _modulate(xm, g2_ref[...], sc2_ref[...], sh2_ref[...])
    h2_ref[...] = h2
    qp_ref[...] = _bdot(h2.astype(jnp.bfloat16), wq_ref[...]).astype(jnp.bfloat16)


def _merge(a, b, sg, x2d, gt, sc2, sh2, g2, wa, wb, wo, wq, *, tm):
    T, D = x2d.shape
    nt = T // tm
    n_mod, R, _ = gt.shape
    tiles_per_mod = nt // n_mod
    const2 = lambda i: (0, 0)
    row = lambda w: pl.BlockSpec((tm, w), lambda i: (i, 0))
    mod_spec = pl.BlockSpec((None, R, D), lambda i: (i // tiles_per_mod, 0, 0))
    nq = wq.shape[1]
    return pl.pallas_call(
        _merge_kernel,
        grid=(nt,),
        in_specs=[row(a.shape[1]), row(b.shape[1]), row(sg.shape[1]), row(D), mod_spec, mod_spec, mod_spec,
                  pl.BlockSpec((1, D), const2)] + [pl.BlockSpec(w.shape, const2) for w in (wa, wb, wo, wq)],
        out_specs=[row(D), row(D), row(nq)],
        out_shape=[jax.ShapeDtypeStruct((T, D), jnp.float32), jax.ShapeDtypeStruct((T, D), jnp.float32),
                   jax.ShapeDtypeStruct((T, nq), jnp.bfloat16)],
        compiler_params=_cparams(("parallel",)),
        name="merge",
    )(a, b, sg, x2d, gt, sc2, sh2, g2, wa, wb, wo, wq)


_PAIR_COUNTS = [PEER_TOPK // (a + 1) for a in range(PEER_TOPK)]
_PAIR_OFFS = [int(v) for v in np.cumsum([0] + _PAIR_COUNTS[:-1])]
_N_PAIRS = sum(_PAIR_COUNTS)
_PAIR_ROWS = -(-_N_PAIRS // SUBLANES) * SUBLANES


def _route_kernel(qp_ref, sk_ref, e_ref, g_ref, s_ref, sv_ref, si_ref, cand_ref, pay_ref, gv_ref):
    TT = qp_ref.shape[0]
    half = PEER_KEY_DIM // 2
    NC = PEER_HEADS * 2
    NEG_INF = -jnp.inf

    for c in range(NC):
        s_ref[c] = _dot_nt(sk_ref[c // 2, c % 2], qp_ref[:, c * half:(c + 1) * half])

    rowk = lax.broadcasted_iota(jnp.int32, (PEER_N_KEYS, TT), 0)

    def key_round(r, carry):
        for c in range(NC):
            s = s_ref[c]
            m = jnp.max(s, axis=0, keepdims=True)
            am = jnp.min(jnp.where(s == m, rowk, PEER_N_KEYS), axis=0, keepdims=True)
            sv_ref[c, pl.ds(r, 1), :] = m
            si_ref[c, pl.ds(r, 1), :] = am
            s_ref[c] = jnp.where(rowk == am, NEG_INF, s)
        return carry

    lax.fori_loop(0, PEER_TOPK, key_round, 0)

    for h in range(PEER_HEADS):
        sv0, sv1 = sv_ref[2 * h], sv_ref[2 * h + 1]
        si0, si1 = si_ref[2 * h], si_ref[2 * h + 1]
        cand_ref[h, _PAIR_ROWS - SUBLANES:_PAIR_ROWS, :] = jnp.full((SUBLANES, TT), NEG_INF, jnp.float32)
        pay_ref[h, _PAIR_ROWS - SUBLANES:_PAIR_ROWS, :] = jnp.zeros((SUBLANES, TT), jnp.int32)
        for a in range(PEER_TOPK):
            n, off = _PAIR_COUNTS[a], _PAIR_OFFS[a]
            cand_ref[h, off:off + n, :] = sv0[a:a + 1, :] + sv1[0:n, :]
            pay_ref[h, off:off + n, :] = si0[a:a + 1, :] * PEER_N_KEYS + si1[0:n, :]

    rowp = lax.broadcasted_iota(jnp.int32, (_PAIR_ROWS, TT), 0)

    def pair_round(r, carry):
        for h in range(PEER_HEADS):
            s = cand_ref[h]
            m = jnp.max(s, axis=0, keepdims=True)
            am = jnp.min(jnp.where(s == m, rowp, _PAIR_ROWS), axis=0, keepdims=True)
            hit = rowp == am
            gv_ref[h, pl.ds(r, 1), :] = m
            e_ref[pl.ds(h * PEER_TOPK + r, 1), :] = jnp.sum(jnp.where(hit, pay_ref[h], 0), axis=0, keepdims=True)
            cand_ref[h] = jnp.where(hit, NEG_INF, s)
        return carry

    lax.fori_loop(0, PEER_TOPK, pair_round, 0)

    for h in range(PEER_HEADS):
        gv = gv_ref[h]
        ex = jnp.exp(gv - jnp.max(gv, axis=0, keepdims=True))
        g_ref[h * PEER_TOPK:(h + 1) * PEER_TOPK, :] = ex / jnp.sum(ex, axis=0, keepdims=True)


def _route(qp, sk16, *, tt):
    T = qp.shape[0]
    nt = T // tt
    ne = PEER_HEADS * PEER_TOPK
    out = jax.ShapeDtypeStruct((nt, ne, tt), jnp.int32)
    blk = pl.BlockSpec((None, ne, tt), lambda i: (i, 0, 0))
    return pl.pallas_call(
        _route_kernel,
        grid=(nt,),
        in_specs=[pl.BlockSpec((tt, qp.shape[1]), lambda i: (i, 0)),
                  pl.BlockSpec(sk16.shape, lambda i: (0, 0, 0, 0))],
        out_specs=[blk, blk],
        out_shape=[out, jax.ShapeDtypeStruct(out.shape, jnp.float32)],
        scratch_shapes=[pltpu.VMEM((PEER_HEADS * 2, PEER_N_KEYS, tt), jnp.float32),
                        pltpu.VMEM((PEER_HEADS * 2, PEER_TOPK, tt), jnp.float32),
                        pltpu.VMEM((PEER_HEADS * 2, PEER_TOPK, tt), jnp.int32),
                        pltpu.VMEM((PEER_HEADS, _PAIR_ROWS, tt), jnp.float32),
                        pltpu.VMEM((PEER_HEADS, _PAIR_ROWS, tt), jnp.int32),
                        pltpu.VMEM((PEER_HEADS, PEER_TOPK, tt), jnp.float32)],
        compiler_params=_cparams(("parallel",)),
        name="peer_route",
    )(qp, sk16)


def _peer_kernel(ids_ref, g_ref, h2_ref, xm_ref, gt_ref, w_hbm, o_ref, *scratch):
    TP, D = h2_ref.shape
    NE = g_ref.shape[1]
    HS = D // 2 // LANES
    NG = TP // PEER_GROUP
    bf = jnp.bfloat16
    wbufs, sem = scratch[:-1], scratch[-1]

    def issue(grp, half):
        for p in range(PEER_GROUP):
            base = (grp * PEER_GROUP + p) * NE
            slot = half * PEER_GROUP + p
            for k in range(NE):
                src = pl.multiple_of(ids_ref[base + k], PEER_ROWS)
                pltpu.make_async_copy(w_hbm.at[pl.ds(src, PEER_ROWS)],
                                      wbufs[slot].at[pl.ds(k * PEER_ROWS, PEER_ROWS)],
                                      sem.at[slot]).start(priority=k % 2)

    def wait(slot):
        pltpu.make_async_copy(w_hbm.at[pl.ds(0, NE * PEER_ROWS)], wbufs[slot], sem.at[slot]).wait()

    def table(slot, r0):
        lo, hi = [], []
        for s in range(HS):
            w = wbufs[slot][pl.ds(r0 + s, NE, stride=PEER_ROWS), :]
            lo.append(pltpu.bitcast(w << 16, jnp.float32).astype(bf))
            hi.append(pltpu.bitcast(w & jnp.int32(-65536), jnp.float32).astype(bf))
        return jnp.concatenate(lo + hi, axis=1)

    for g0 in range(PEER_SETS - 1):
        issue(g0, g0)

    def ring(i, carry):
        for r in range(PEER_SETS):
            group(PEER_SETS * i + r, r, True)
        return carry

    def group(grp, half, issue_ahead):
        slots = [half * PEER_GROUP + p for p in range(PEER_GROUP)]
        toks = [grp * PEER_GROUP + p for p in range(PEER_GROUP)]
        for slot in slots:
            wait(slot)
        if issue_ahead:
            issue(grp + PEER_SETS - 1, (half + PEER_SETS - 1) % PEER_SETS)
        acts = []
        for t, slot in zip(toks, slots):
            x = jnp.broadcast_to(h2_ref[pl.ds(t, 1), :], (SUBLANES, D)).astype(bf)
            acts.append(_dot_nt(x, table(slot, 0))[0:1, :])
        ys = []
        for t, slot, act in zip(toks, slots, acts):
            c = g_ref[pl.ds(t, 1), :] * jax.nn.gelu(act)
            c8 = jnp.broadcast_to(c, (SUBLANES, NE)).astype(bf)
            ys.append(_bdot(c8, table(slot, HS))[0:1, :])
        for t, y in zip(toks, ys):
            gt = gt_ref[pl.ds(t, 1), :] if gt_ref.shape[0] == TP else gt_ref[...]
            o_ref[pl.ds(t, 1), :] = xm_ref[pl.ds(t, 1), :] + gt * y

    n_ring = (NG - (PEER_SETS - 1)) // PEER_SETS
    lax.fori_loop(0, n_ring, ring, 0)
    for grp in range(n_ring * PEER_SETS, NG):
        group(grp, grp % PEER_SETS, grp + PEER_SETS - 1 < NG)


def _peer(ids, g, h2, xm, gt, wtab, *, tp):
    T, D = h2.shape
    NE = ids.shape[1]
    nt = T // tp
    n_mod, R, _ = gt.shape
    tiles_per_mod = nt // n_mod
    row = lambda w: pl.BlockSpec((tp, w), lambda i: (i, 0))
    return pl.pallas_call(
        _peer_kernel,
        grid=(nt,),
        in_specs=[pl.BlockSpec((tp * NE,), lambda i: (i,), memory_space=pltpu.SMEM),
                  row(NE), row(D), row(D),
                  pl.BlockSpec((None, R, D), lambda i: (i // tiles_per_mod, 0, 0)),
                  pl.BlockSpec(memory_space=pl.ANY)],
        out_specs=row(D),
        out_shape=jax.ShapeDtypeStruct((T, D), jnp.float32),
        scratch_shapes=[pltpu.VMEM((NE * PEER_ROWS, LANES), jnp.int32)] * (PEER_SETS * PEER_GROUP)
        + [pltpu.SemaphoreType.DMA((PEER_SETS * PEER_GROUP,))],
        compiler_params=_cparams(("arbitrary",)),
        name="peer_gather",
    )(ids.reshape(-1), g, h2, xm, gt, wtab)


def _pack_bf16_pairs(t):
    E, D = t.shape
    b = lax.bitcast_convert_type(t.astype(jnp.bfloat16), jnp.uint16).astype(jnp.uint32)
    w = b[:, :D // 2] | (b[:, D // 2:] << 16)
    return lax.bitcast_convert_type(w, jnp.int32).reshape(E, D // 2 // LANES, LANES)


def _rope_tables(pos, rot_dim, width):
    half = rot_dim // 2
    inv_freq = ROPE_THETA ** (-jnp.arange(half, dtype=jnp.float32) / half)
    ang = pos.astype(jnp.float32)[..., None] * inv_freq
    cos, sin = jnp.cos(ang), jnp.sin(ang)
    w = np.arange(LANES) % width
    first, second = w < half, (w >= half) & (w < rot_dim)
    src = np.where(first, w, np.where(second, w - half, 0))
    c = jnp.where(first | second, cos[:, src], 1.0)
    sa = jnp.where(first, -sin[:, src], 0.0)
    sb = jnp.where(second, sin[:, src], 0.0)
    return c, sa, sb


def _gmlp_mask():
    i = np.arange(GMLP_CHUNK)
    return (i[None, :] // CHUNK) <= (i[:, None] // CHUNK)


def _rows_per_seq(v, reps, tm):
    nb, d = v.shape
    return jnp.repeat(v, reps, axis=0).reshape(nb * reps // tm, tm, d)


def _layer(xp, xs, cache_k, cache_v, cache_kidx, c_prompt, c_sample, ada_w, ada_b, norm1_g, norm2_g, w_in,
           q_norm_g, k_norm_g, kidx_norm_g, gmlp_v_norm_g, gmlp_ws, gmlp_b, w_branch_a, w_branch_b, w_out,
           peer_wq, peer_subkeys, peer_u, peer_v):
    B, S, D = xp.shape
    DB, DS, _ = xs.shape
    P = cache_k.shape[1]
    Tp, Ts = B * S, DB * DS
    tm_p, tm_s = 2 * GMLP_CHUNK, GMLP_CHUNK
    assert S % tm_p == 0 and GMLP_CHUNK % DS == 0 and Ts % tm_s == 0 and DS <= CHUNK and P % CHUNK == 0
    tp_p, tp_s = min(PEER_TOK, S), min(PEER_TOK, Ts)
    assert S % tp_p == 0 and Ts % tp_s == 0
    assert tp_p % PEER_GROUP == 0 and tp_s % PEER_GROUP == 0 and min(tp_p, tp_s) // PEER_GROUP >= PEER_SETS
    bf = jnp.bfloat16
    f32 = jnp.float32

    nrow = B + DB
    npad = -nrow % SUBLANES
    c_all = jnp.concatenate([c_prompt, c_sample, jnp.zeros((npad, D), f32)], axis=0)
    mod = _adaln(c_all, ada_w, ada_b)
    mods = jnp.split(mod, 6, axis=-1)
    mp = [m[:B].reshape(B, 1, D) for m in mods]
    ms = [_rows_per_seq(m[B:B + DB], DS, tm_s) for m in mods]

    sizes = (GMLP_WIDTH, GMLP_WIDTH, N_HEADS * HEAD_DIM, N_KV_HEADS * HEAD_DIM, N_KV_HEADS * HEAD_DIM,
             N_IDX_HEADS * IDX_DIM, IDX_DIM, N_IDX_HEADS, 2 * D)
    pts = [int(s) for s in np.cumsum(sizes)[:-1]]
    wu, wv, wq, wk, wva, wqi, wki, wwi, wg = jnp.split(w_in.astype(bf), pts, axis=-1)
    wkw = jnp.concatenate([wki, wwi, jnp.zeros((D, LANES - IDX_DIM - N_IDX_HEADS), bf)], axis=-1)
    wts = (wu, wv, wq, wk, wva, wqi, wg, wkw)
    kig = jnp.concatenate([kidx_norm_g, jnp.zeros((LANES - IDX_DIM,), f32)]).reshape(1, LANES)
    gains = (q_norm_g.reshape(1, HEAD_DIM), k_norm_g.reshape(1, HEAD_DIM), kig, gmlp_v_norm_g.reshape(1, GMLP_WIDTH))
    g1 = norm1_g.reshape(1, D)
    g2 = norm2_g.reshape(1, D)

    wmask = jnp.where(_gmlp_mask()[None], gmlp_ws, 0.0)
    wm_p = wmask.astype(bf)
    gb_p = jnp.repeat(gmlp_b.T, GMLP_GROUP_DIM, axis=1)
    reps = GMLP_CHUNK // DS
    eye = jnp.eye(reps, dtype=f32)
    wm_s = jnp.einsum("ab,gij->gaibj", eye, wmask[:, :DS, :DS]).reshape(GMLP_GROUPS, GMLP_CHUNK, GMLP_CHUNK).astype(bf)
    gb_s = jnp.tile(gb_p[:DS], (reps, 1))

    pos_p = jnp.arange(S)
    pos_s = P + (jnp.arange(tm_s) % DS)
    tabs_p = _rope_tables(pos_p, ROT_DIM, HEAD_DIM) + _rope_tables(pos_p, IDX_ROT_DIM, IDX_DIM)
    tabs_s = _rope_tables(pos_s, ROT_DIM, HEAD_DIM) + _rope_tables(pos_s, IDX_ROT_DIM, IDX_DIM)

    x2p = xp.reshape(Tp, D)
    x2s = xs.reshape(Ts, D)

    (a_p, q_p, k32_p, k16_p, va32_p, va16_p, qi_p, kw32_p, ki16_p, sg_p) = _inproj(
        x2p, mp[1], mp[0], g1, wts, tabs_p, gains, wm_p, gb_p, tm=tm_p, tab_blocks=S // tm_p, emit_v=False)
    wit_p = kw32_p[:, IDX_DIM:IDX_DIM + N_IDX_HEADS].reshape(B, S, N_IDX_HEADS).transpose(0, 2, 1)
    vt_p = va16_p.reshape(B, S, -1).transpose(0, 2, 1)
    b_p = _dsa(qi_p, wit_p, q_p, ki16_p.reshape(B, S, LANES), k16_p.reshape(B, S, -1), vt_p,
               ksel=min(TOPK_MAX, S // 4), kt=min(KT_PROMPT, S), vis_step=QBLK, vis_base=QBLK,
               lim_lo=CHUNK, lim_hi=2 * CHUNK)

    (a_s, v_s, q_s, k32_s, k16_s, va32_s, va16_s, qi_s, kw32_s, ki16_s, sg_s) = _inproj(
        x2s, ms[1], ms[0], g1, wts, tabs_s, gains, wm_s, gb_s, tm=tm_s, tab_blocks=1, emit_v=True)
    Lk = P + DS
    Lpad = -(-Lk // KT_SAMPLE) * KT_SAMPLE

    def pad_q(a):
        w = a.shape[-1]
        return jnp.pad(a.reshape(DB, DS, w), ((0, 0), (0, QBLK - DS), (0, 0))).reshape(DB * QBLK, w)

    def cat_keys(cache, new, w):
        return jnp.concatenate([cache.reshape(DB, P, -1).astype(bf), new.reshape(DB, DS, -1),
                                jnp.zeros((DB, Lpad - Lk, w), bf)], axis=1)

    wit_s = jnp.pad(kw32_s[:, IDX_DIM:IDX_DIM + N_IDX_HEADS].reshape(DB, DS, N_IDX_HEADS).transpose(0, 2, 1),
                    ((0, 0), (0, 0), (0, QBLK - DS)))
    kidx_cache = jnp.pad(cache_kidx, ((0, 0), (0, 0), (0, LANES - IDX_DIM)))
    ki_all = cat_keys(kidx_cache, ki16_s, LANES)
    k_all = cat_keys(cache_k, k16_s, N_KV_HEADS * HEAD_DIM)
    vt_all = cat_keys(cache_v, va16_s, N_KV_HEADS * HEAD_DIM).transpose(0, 2, 1)
    b_s = _dsa(pad_q(qi_s), wit_s, pad_q(q_s), ki_all, k_all, vt_all,
               ksel=min(TOPK_MAX, Lk // 4), kt=KT_SAMPLE, vis_step=0, vis_base=Lpad, lim_lo=Lk, lim_hi=Lk,
               q_valid=DS)
    b_s = b_s.reshape(DB, QBLK, -1)[:, :DS].reshape(Ts, -1)

    wa, wb, wo, wpq = w_branch_a.astype(bf), w_branch_b.astype(bf), w_out.astype(bf), peer_wq.astype(bf)
    sk16 = peer_subkeys.astype(bf)
    wtab = jnp.concatenate([_pack_bf16_pairs(peer_u), _pack_bf16_pairs(peer_v)], axis=1).reshape(-1, LANES)

    def tail(a, b, sg, x2d, m, tm):
        xm, h2, qp = _merge(a, b, sg, x2d, m[2], m[4], m[3], g2, wa, wb, wo, wpq, tm=tm)
        e, g = _route(qp, sk16, tt=LANES)
        T = x2d.shape[0]
        ids = e.transpose(0, 2, 1).reshape(T, PEER_HEADS * PEER_TOPK) * PEER_ROWS
        gw = g.transpose(0, 2, 1).reshape(T, PEER_HEADS * PEER_TOPK)
        return ids, gw, h2, xm

    ids_p, gw_p, h2_p, xm_p = tail(a_p, b_p, sg_p, x2p, mp, tm_p)
    out_p = _peer(ids_p, gw_p, h2_p, xm_p, mp[5], wtab, tp=tp_p)
    ms_peer = [_rows_per_seq(m[B:B + DB], DS, tp_s) for m in mods]
    ids_s, gw_s, h2_s, xm_s = tail(a_s, b_s, sg_s, x2s, ms, tm_s)
    out_s = _peer(ids_s, gw_s, h2_s, xm_s, ms_peer[5], wtab, tp=tp_s)

    new = (k32_p.reshape(B, S, N_KV_HEADS, HEAD_DIM), va32_p.reshape(B, S, N_KV_HEADS, HEAD_DIM),
           kw32_p[:, :IDX_DIM].reshape(B, S, IDX_DIM),
           k32_s.reshape(DB, DS, N_KV_HEADS, HEAD_DIM), va32_s.reshape(DB, DS, N_KV_HEADS, HEAD_DIM),
           kw32_s[:, :IDX_DIM].reshape(DB, DS, IDX_DIM), v_s.reshape(DB, DS, GMLP_WIDTH))
    return out_p.reshape(B, S, D), out_s.reshape(DB, DS, D), new


def kernel(x_prompt, x_sample, cache_k, cache_v, cache_kidx, c_prompt, c_sample, ada_w, ada_b, norm1_g, norm2_g,
           w_in, q_norm_g, k_norm_g, kidx_norm_g, gmlp_v_norm_g, gmlp_ws, gmlp_b, w_branch_a, w_branch_b, w_out,
           peer_wq, peer_subkeys, peer_u, peer_v):
    xp, xs = x_prompt, x_sample
    per_layer = []
    for l in range(ada_w.shape[0]):
        xp, xs, new = _layer(xp, xs, cache_k[l], cache_v[l], cache_kidx[l], c_prompt, c_sample, ada_w[l], ada_b[l],
                             norm1_g[l], norm2_g[l], w_in[l], q_norm_g[l], k_norm_g[l], kidx_norm_g[l],
                             gmlp_v_norm_g[l], gmlp_ws[l], gmlp_b[l], w_branch_a[l], w_branch_b[l], w_out[l],
                             peer_wq[l], peer_subkeys[l], peer_u[l], peer_v[l])
        per_layer.append(new)
    stacked = tuple(jnp.stack([n[i] for n in per_layer]) for i in range(7))
    return (xp, xs) + stacked
```

```python
import functools

import jax
import jax.numpy as jnp
import numpy as np
from jax import lax
from jax.experimental import pallas as pl
from jax.experimental.pallas import tpu as pltpu

CHUNK = 64
EPS = 1e-6
ROPE_THETA = 500000.0
GMLP_CHUNK = 128
GMLP_GROUPS = 8
GMLP_GROUP_DIM = 128
GMLP_WIDTH = GMLP_GROUPS * GMLP_GROUP_DIM
N_HEADS = 8
N_KV_HEADS = 2
Q_PER_KV = N_HEADS // N_KV_HEADS
HEAD_DIM = 128
ROT_DIM = HEAD_DIM // 4
N_IDX_HEADS = 8
IDX_DIM = 64
IDX_ROT_DIM = IDX_DIM // 4
TOPK_MAX = 256
PEER_HEADS = 8
PEER_N_KEYS = 128
PEER_KEY_DIM = 256
PEER_TOPK = 16

LANES = 128
SUBLANES = 8
VMEM_LIMIT_BYTES = 56 * 1024 * 1024

QBLK = LANES
KT_PROMPT = 1024
KT_SAMPLE = 256
KT_COUNT = 512
QK_FOLD = HEAD_DIM ** -0.5 * float(np.log2(np.e))
TOPK_PASS_STAGES = (24, 28, 32)
SOFTMAX_CEIL_MAX = 50.0
PEER_TOK = 256
PEER_SETS = 3
PEER_GROUP = 4
PEER_ROWS = 8
INT_MIN = -(2 ** 31)
NEG_BIG = -1e30
IDX_BIG = 2 ** 30


def _cparams(sem):
    return pltpu.CompilerParams(dimension_semantics=sem, vmem_limit_bytes=VMEM_LIMIT_BYTES)


def _rms(x, width):
    return x * lax.rsqrt(jnp.sum(x * x, axis=-1, keepdims=True) * (1.0 / width) + EPS)


def _modulate(x, g, sc, sh):
    return _rms(x, x.shape[-1]) * g * (1.0 + sc) + sh


def _rope(x, c, sa, sb, half):
    n = x.shape[-1]
    return x * c + pltpu.roll(x, n - half, 1) * sa + pltpu.roll(x, half, 1) * sb


def _bdot(a, b):
    return jnp.dot(a, b, preferred_element_type=jnp.float32)


def _dot_nt(a, b):
    return lax.dot_general(a, b, (((1,), (1,)), ((), ())), preferred_element_type=jnp.float32)


def _adaln_kernel(c_ref, w_ref, b_ref, o_ref):
    c = c_ref[...]
    h = (c * jax.nn.sigmoid(c)).astype(jnp.bfloat16)
    o_ref[...] = _bdot(h, w_ref[...].astype(jnp.bfloat16)) + b_ref[...]


def _adaln(c, w, b):
    rows, d = c.shape
    n = w.shape[1]
    tn = n // 4
    return pl.pallas_call(
        _adaln_kernel,
        grid=(n // tn,),
        in_specs=[pl.BlockSpec((rows, d), lambda j: (0, 0)),
                  pl.BlockSpec((d, tn), lambda j: (0, j)),
                  pl.BlockSpec((1, tn), lambda j: (0, j))],
        out_specs=pl.BlockSpec((rows, tn), lambda j: (0, j)),
        out_shape=jax.ShapeDtypeStruct((rows, n), jnp.float32),
        compiler_params=_cparams(("arbitrary",)),
        name="adaln",
    )(c, w, b.reshape(1, n))


def _inproj_kernel(x_ref, sc_ref, sh_ref, g1_ref,
                   wu_ref, wv_ref, wq_ref, wk_ref, wva_ref, wqi_ref, wg_ref, wkw_ref,
                   c128_ref, sa128_ref, sb128_ref, c64_ref, sa64_ref, sb64_ref,
                   qg_ref, kg_ref, kig_ref, gvg_ref, wm_ref, gb_ref,
                   *out_refs, emit_v):
    if emit_v:
        (a_ref, v_ref, q_ref, k32_ref, k16_ref, va32_ref, va16_ref, qi_ref, kw32_ref, ki16_ref, sg_ref) = out_refs
    else:
        (a_ref, q_ref, k32_ref, k16_ref, va32_ref, va16_ref, qi_ref, kw32_ref, ki16_ref, sg_ref) = out_refs
        v_ref = None
    tm = x_ref.shape[0]
    h = _modulate(x_ref[...], g1_ref[...], sc_ref[...], sh_ref[...]).astype(jnp.bfloat16)

    c128, sa128, sb128 = c128_ref[...], sa128_ref[...], sb128_ref[...]
    c64, sa64, sb64 = c64_ref[...], sa64_ref[...], sb64_ref[...]

    v = _rms(jax.nn.gelu(_bdot(h, wv_ref[...])), GMLP_WIDTH) * gvg_ref[...]
    if v_ref is not None:
        v_ref[...] = v
    v16 = v.astype(jnp.bfloat16)
    u = jax.nn.gelu(_bdot(h, wu_ref[...]))
    for c in range(tm // GMLP_CHUNK):
        rows = slice(c * GMLP_CHUNK, (c + 1) * GMLP_CHUNK)
        for g in range(GMLP_GROUPS):
            cols = slice(g * GMLP_GROUP_DIM, (g + 1) * GMLP_GROUP_DIM)
            s = _bdot(wm_ref[g], v16[rows, cols]) + gb_ref[:, cols]
            a_ref[rows, cols] = (u[rows, cols] * s).astype(jnp.bfloat16)

    q = _bdot(h, wq_ref[...])
    for hd in range(N_HEADS):
        cols = slice(hd * HEAD_DIM, (hd + 1) * HEAD_DIM)
        qh = _rope(_rms(q[:, cols], HEAD_DIM) * qg_ref[...], c128, sa128, sb128, ROT_DIM // 2)
        q_ref[:, cols] = (qh * QK_FOLD).astype(jnp.bfloat16)
    k = _bdot(h, wk_ref[...])
    for hd in range(N_KV_HEADS):
        cols = slice(hd * HEAD_DIM, (hd + 1) * HEAD_DIM)
        kh = _rope(_rms(k[:, cols], HEAD_DIM) * kg_ref[...], c128, sa128, sb128, ROT_DIM // 2)
        k32_ref[:, cols] = kh
        k16_ref[:, cols] = kh.astype(jnp.bfloat16)
    va = _bdot(h, wva_ref[...])
    va32_ref[...] = va
    va16_ref[...] = va.astype(jnp.bfloat16)

    qi = _bdot(h, wqi_ref[...])
    for p in range(N_IDX_HEADS * IDX_DIM // LANES):
        cols = slice(p * LANES, (p + 1) * LANES)
        qi_ref[:, cols] = _rope(qi[:, cols], c64, sa64, sb64, IDX_ROT_DIM // 2).astype(jnp.bfloat16)
    kw = _bdot(h, wkw_ref[...])
    is_ki = lax.broadcasted_iota(jnp.int32, kw.shape, 1) < IDX_DIM
    kis = jnp.where(is_ki, kw, 0.0)
    kin = kis * lax.rsqrt(jnp.sum(kis * kis, axis=-1, keepdims=True) * (1.0 / IDX_DIM) + EPS) * kig_ref[...]
    kir = _rope(kin, c64, sa64, sb64, IDX_ROT_DIM // 2)
    kw32_ref[...] = jnp.where(is_ki, kir, kw * (N_IDX_HEADS ** -0.5))
    ki16_ref[...] = jnp.where(is_ki, kir, 0.0).astype(jnp.bfloat16)

    sg_ref[...] = jax.nn.sigmoid(_bdot(h, wg_ref[...])).astype(jnp.bfloat16)


def _inproj(x2d, sc, sh, g1, wts, tabs, gains, wm, gb, *, tm, tab_blocks, emit_v):
    T, D = x2d.shape
    nt = T // tm
    n_mod, R, _ = sc.shape
    tiles_per_mod = nt // n_mod
    const2 = lambda i: (0, 0)
    mod_spec = pl.BlockSpec((None, R, D), lambda i: (i // tiles_per_mod, 0, 0))
    tab_spec = pl.BlockSpec((tm, LANES), lambda i: (i % tab_blocks, 0))
    in_specs = ([pl.BlockSpec((tm, D), lambda i: (i, 0)), mod_spec, mod_spec, pl.BlockSpec((1, D), const2)]
                + [pl.BlockSpec(w.shape, const2) for w in wts]
                + [tab_spec] * 6
                + [pl.BlockSpec(g.shape, const2) for g in gains]
                + [pl.BlockSpec(wm.shape, lambda i: (0, 0, 0)), pl.BlockSpec(gb.shape, const2)])
    widths = [(GMLP_WIDTH, jnp.bfloat16)]
    if emit_v:
        widths.append((GMLP_WIDTH, jnp.float32))
    widths += [(N_HEADS * HEAD_DIM, jnp.bfloat16),
               (N_KV_HEADS * HEAD_DIM, jnp.float32), (N_KV_HEADS * HEAD_DIM, jnp.bfloat16),
               (N_KV_HEADS * HEAD_DIM, jnp.float32), (N_KV_HEADS * HEAD_DIM, jnp.bfloat16),
               (N_IDX_HEADS * IDX_DIM, jnp.bfloat16),
               (LANES, jnp.float32), (LANES, jnp.bfloat16),
               (2 * D, jnp.bfloat16)]
    return pl.pallas_call(
        functools.partial(_inproj_kernel, emit_v=emit_v),
        grid=(nt,),
        in_specs=in_specs,
        out_specs=[pl.BlockSpec((tm, w), lambda i: (i, 0)) for w, _ in widths],
        out_shape=[jax.ShapeDtypeStruct((T, w), dt) for w, dt in widths],
        compiler_params=_cparams(("parallel",)),
        name="inproj_v" if emit_v else "inproj",
    )(x2d, sc, sh, g1, *wts, *tabs, *gains, wm, gb)


def _dsa_kernel(qi_ref, wi_ref, q_ref, ki_ref, k_ref, vt_ref, o_ref,
                keys_ref, bias_ref, meff_ref, acc_ref, kmax_ref, l_ref, tu_ref,
                *, ksel, KT, vis_step, vis_base, lim_lo, lim_hi, q_valid):
    j = pl.program_id(1)
    L = ki_ref.shape[0]
    ntiles = jnp.minimum((j * vis_step + vis_base + KT - 1) // KT, L // KT)
    lane = lax.broadcasted_iota(jnp.int32, (1, QBLK), 1)
    limit = j * vis_step + jnp.where(lane < CHUNK, lim_lo, lim_hi)
    if q_valid < QBLK:
        limit = jnp.where(lane < q_valid, limit, 0)
    row = lax.broadcasted_iota(jnp.int32, (KT, QBLK), 0)

    def tile_off(t):
        return pl.multiple_of(t * KT, KT)

    wi = wi_ref[...] * (IDX_DIM ** -0.5)
    qit = qi_ref[...].astype(jnp.float32).T.astype(jnp.bfloat16)
    qit = jnp.concatenate([qit[h * IDX_DIM:(h + 1) * IDX_DIM, :] for h in range(N_IDX_HEADS)], axis=1)
    qit = jnp.concatenate([qit, jnp.zeros((LANES - IDX_DIM, N_IDX_HEADS * QBLK), jnp.bfloat16)], axis=0)

    def idx_body(t, carry):
        off = tile_off(t)
        s = _bdot(ki_ref[pl.ds(off, KT), :], qit)
        acc = jnp.zeros((KT, QBLK), jnp.float32)
        for h in range(N_IDX_HEADS):
            acc = acc + jnp.maximum(s[:, h * QBLK:(h + 1) * QBLK], 0.0) * wi[h:h + 1, :]
        bits = pltpu.bitcast(acc, jnp.int32)
        key = jnp.where(bits >= 0, bits, bits ^ 0x7FFFFFFF)
        keys_ref[pl.ds(off, KT), :] = jnp.where(row + off < limit, key, INT_MIN)
        return carry

    lax.fori_loop(0, ntiles, idx_body, 0)

    KC = min(KT, KT_COUNT)
    nctiles = jnp.minimum((j * vis_step + vis_base + KC - 1) // KC, L // KC)
    rowc = lax.broadcasted_iota(jnp.int32, (KC, QBLK), 0)

    def count(indicator):
        def body(t, acc):
            off = pl.multiple_of(t * KC, KC)
            ones = indicator(keys_ref[pl.ds(off, KC), :], off)
            return acc + ones.reshape(KC // SUBLANES, SUBLANES, QBLK).sum(axis=0)
        acc = lax.fori_loop(0, nctiles, body, jnp.zeros((SUBLANES, QBLK), jnp.int32))
        return acc.sum(axis=0, keepdims=True)

    def bit_body(i, state):
        tu, settled = state
        cand_u = tu | lax.shift_left(jnp.int32(1), 31 - i)
        cand_s = cand_u ^ INT_MIN
        cnt = count(lambda key, off: jnp.where(key >= cand_s, 1, 0))
        tu = jnp.where(settled > 0, tu, jnp.where(cnt >= ksel, cand_u, tu))
        return tu, jnp.where(cnt == ksel, 1, settled)

    def park(state):
        tu_ref[0:1, :] = state[0]
        tu_ref[1:2, :] = state[1]

    zero_row = jnp.zeros((1, QBLK), jnp.int32)
    park(lax.fori_loop(0, TOPK_PASS_STAGES[0], bit_body, (zero_row, zero_row)))
    for lo, hi in zip(TOPK_PASS_STAGES[:-1], TOPK_PASS_STAGES[1:]):
        @pl.when(jnp.min(tu_ref[1:2, :]) == 0)
        def _(lo=lo, hi=hi):
            park(lax.fori_loop(lo, hi, bit_body, (tu_ref[0:1, :], tu_ref[1:2, :])))

    tu = tu_ref[0:1, :]
    thr = jnp.maximum(tu ^ INT_MIN, INT_MIN + 1)

    need = count(lambda key, off: jnp.where(key >= thr, 1, 0)) > ksel
    meff_ref[...] = jnp.full(meff_ref.shape, IDX_BIG, jnp.int32)

    @pl.when(jnp.max(need.astype(jnp.int32)) > 0)
    def _():
        want = ksel - count(lambda key, off: jnp.where(key > thr, 1, 0))
        nbits = int(L).bit_length()

        def m_body(i, mp):
            cand = mp | lax.shift_left(jnp.int32(1), nbits - 1 - i)
            f = count(lambda key, off: jnp.where(key == thr, jnp.where(rowc + off < cand, 1, 0), 0))
            return jnp.where(f < want, cand, mp)

        mp = lax.fori_loop(0, nbits, m_body, jnp.zeros((1, QBLK), jnp.int32))
        meff_ref[...] = jnp.broadcast_to(jnp.where(need, mp, IDX_BIG), meff_ref.shape)

    meff = meff_ref[0:1, :]

    def bias_body(t, carry):
        off = tile_off(t)
        key = keys_ref[pl.ds(off, KT), :]
        tie = jnp.where(row + off <= meff, 0.0, NEG_BIG)
        bias_ref[pl.ds(off, KT), :] = jnp.where(key == thr, tie, jnp.where(key > thr, 0.0, NEG_BIG))
        return carry

    lax.fori_loop(0, ntiles, bias_body, 0)

    @pl.when(j == 0)
    def _():
        ones_c = jnp.ones((HEAD_DIM, LANES), jnp.bfloat16)

        def kn_body(t, carry):
            off = tile_off(t)
            out = []
            for g in range(N_KV_HEADS):
                kf = k_ref[pl.ds(off, KT), g * HEAD_DIM:(g + 1) * HEAD_DIM].astype(jnp.float32)
                n2 = _bdot((kf * kf).astype(jnp.bfloat16), ones_c)
                out.append(jnp.maximum(carry[g], jnp.max(n2, axis=0, keepdims=True)))
            return tuple(out)

        kn = lax.fori_loop(0, L // KT, kn_body, (jnp.zeros((1, LANES), jnp.float32),) * N_KV_HEADS)
        for g in range(N_KV_HEADS):
            kmax_ref[g:g + 1, :] = kn[g]

    qgs = [jnp.concatenate([q_ref[:, (g * Q_PER_KV + h) * HEAD_DIM:(g * Q_PER_KV + h + 1) * HEAD_DIM]
                            for h in range(Q_PER_KV)], axis=0) for g in range(N_KV_HEADS)]
    acc_ref[...] = jnp.zeros(acc_ref.shape, jnp.float32)
    W = Q_PER_KV * QBLK

    def att_body(t, carry):
        off = tile_off(t)
        b = bias_ref[pl.ds(off, KT), :]
        b4 = jnp.concatenate([b] * Q_PER_KV, axis=1)
        out = []
        for g in range(N_KV_HEADS):
            m_old, l_old = carry[2 * g], carry[2 * g + 1]
            kt = k_ref[pl.ds(off, KT), g * HEAD_DIM:(g + 1) * HEAD_DIM]
            s = _dot_nt(kt, qgs[g]) + b4
            m_new = jnp.maximum(m_old, jnp.max(s, axis=0, keepdims=True))
            alpha = jnp.exp2(m_old - m_new)
            p = jnp.exp2(s - m_new)
            l_new = alpha * l_old + jnp.sum(p, axis=0, keepdims=True)
            vt = vt_ref[g * HEAD_DIM:(g + 1) * HEAD_DIM, pl.ds(off, KT)]
            acc_ref[g] = acc_ref[g] * alpha + _bdot(vt, p.astype(jnp.bfloat16))
            out += [m_new, l_new]
        return tuple(out)

    ones_l = jnp.ones((SUBLANES, HEAD_DIM), jnp.bfloat16)
    ceil = []
    for g in range(N_KV_HEADS):
        qf = qgs[g].astype(jnp.float32)
        qn2 = _dot_nt(ones_l, (qf * qf).astype(jnp.bfloat16))[0:1, :]
        kn2 = jnp.concatenate([kmax_ref[g:g + 1, :]] * Q_PER_KV, axis=1)
        ceil.append(jnp.sqrt(qn2 * kn2) * 1.02)
    small = jnp.max(jnp.maximum(ceil[0], ceil[1])) <= SOFTMAX_CEIL_MAX

    def fixed_body(t, carry):
        off = tile_off(t)
        b = bias_ref[pl.ds(off, KT), :]
        b4 = jnp.concatenate([b] * Q_PER_KV, axis=1)
        out = []
        for g in range(N_KV_HEADS):
            kt = k_ref[pl.ds(off, KT), g * HEAD_DIM:(g + 1) * HEAD_DIM]
            p = jnp.exp2(_dot_nt(kt, qgs[g]) + (b4 - ceil[g]))
            vt = vt_ref[g * HEAD_DIM:(g + 1) * HEAD_DIM, pl.ds(off, KT)]
            acc_ref[g] = acc_ref[g] + _bdot(vt, p.astype(jnp.bfloat16))
            out.append(carry[g] + jnp.sum(p, axis=0, keepdims=True))
        return tuple(out)

    @pl.when(small)
    def _():
        sums = lax.fori_loop(0, ntiles, fixed_body, (jnp.zeros((1, W), jnp.float32),) * N_KV_HEADS)
        for g in range(N_KV_HEADS):
            l_ref[g, 0:1, :] = sums[g]

    @pl.when(jnp.logical_not(small))
    def _():
        init = (jnp.full((1, W), NEG_BIG, jnp.float32), jnp.zeros((1, W), jnp.float32)) * N_KV_HEADS
        fin = lax.fori_loop(0, ntiles, att_body, init)
        for g in range(N_KV_HEADS):
            l_ref[g, 0:1, :] = fin[2 * g + 1]

    for g in range(N_KV_HEADS):
        l = l_ref[g, 0:1, :]
        o = acc_ref[g] * (1.0 / jnp.where(l > 0.0, l, 1.0))
        for h in range(Q_PER_KV):
            hd = g * Q_PER_KV + h
            o_ref[:, hd * HEAD_DIM:(hd + 1) * HEAD_DIM] = o[:, h * QBLK:(h + 1) * QBLK].T.astype(jnp.bfloat16)


def _dsa(qi, wit, q, ki16, k16, vt, *, ksel, kt, vis_step, vis_base, lim_lo, lim_hi, q_valid=QBLK):
    NB, L, _ = ki16.shape
    Sq = wit.shape[2]
    nq = Sq // QBLK
    assert L % kt == 0
    kern = functools.partial(_dsa_kernel, ksel=ksel, KT=kt, vis_step=vis_step, vis_base=vis_base,
                             lim_lo=lim_lo, lim_hi=lim_hi, q_valid=q_valid)
    return pl.pallas_call(
        kern,
        grid=(NB, nq),
        in_specs=[pl.BlockSpec((QBLK, qi.shape[1]), lambda b, j: (b * nq + j, 0)),
                  pl.BlockSpec((None, N_IDX_HEADS, QBLK), lambda b, j: (b, 0, j)),
                  pl.BlockSpec((QBLK, q.shape[1]), lambda b, j: (b * nq + j, 0)),
                  pl.BlockSpec((None, L, LANES), lambda b, j: (b, 0, 0)),
                  pl.BlockSpec((None, L, k16.shape[2]), lambda b, j: (b, 0, 0)),
                  pl.BlockSpec((None, vt.shape[1], L), lambda b, j: (b, 0, 0))],
        out_specs=pl.BlockSpec((QBLK, q.shape[1]), lambda b, j: (b * nq + j, 0)),
        out_shape=jax.ShapeDtypeStruct(q.shape, jnp.bfloat16),
        scratch_shapes=[pltpu.VMEM((L, QBLK), jnp.int32),
                        pltpu.VMEM((L, QBLK), jnp.float32),
                        pltpu.VMEM((SUBLANES, QBLK), jnp.int32),
                        pltpu.VMEM((N_KV_HEADS, HEAD_DIM, Q_PER_KV * QBLK), jnp.float32),
                        pltpu.VMEM((SUBLANES, LANES), jnp.float32),
                        pltpu.VMEM((N_KV_HEADS, SUBLANES, Q_PER_KV * QBLK), jnp.float32),
                        pltpu.VMEM((SUBLANES, QBLK), jnp.int32)],
        compiler_params=_cparams(("parallel", "arbitrary")),
        name="dsa",
    )(qi, wit, q, ki16, k16, vt)


def _merge_kernel(a_ref, b_ref, sg_ref, x_ref, gt_ref, sc2_ref, sh2_ref, g2_ref,
                  wa_ref, wb_ref, wo_ref, wq_ref, xm_ref, h2_ref, qp_ref):
    D = x_ref.shape[1]
    sg = sg_ref[...].astype(jnp.float32)
    merged = sg[:, :D] * _bdot(a_ref[...], wa_ref[...]) + sg[:, D:] * _bdot(b_ref[...], wb_ref[...])
    y = _bdot(merged.astype(jnp.bfloat16), wo_ref[...])
    xm = x_ref[...] + gt_ref[...] * y
    xm_ref[...] = xm
    h2 = _modulate(xm, g2_ref[...], sc2_ref[...], sh2_ref[...])
    h2_ref[...] = h2
    qp_ref[...] = _bdot(h2.astype(jnp.bfloat16), wq_ref[...]).astype(jnp.bfloat16)


def _merge(a, b, sg, x2d, gt, sc2, sh2, g2, wa, wb, wo, wq, *, tm):
    T, D = x2d.shape
    nt = T // tm
    n_mod, R, _ = gt.shape
    tiles_per_mod = nt // n_mod
    const2 = lambda i: (0, 0)
    row = lambda w: pl.BlockSpec((tm, w), lambda i: (i, 0))
    mod_spec = pl.BlockSpec((None, R, D), lambda i: (i // tiles_per_mod, 0, 0))
    nq = wq.shape[1]
    return pl.pallas_call(
        _merge_kernel,
        grid=(nt,),
        in_specs=[row(a.shape[1]), row(b.shape[1]), row(sg.shape[1]), row(D), mod_spec, mod_spec, mod_spec,
                  pl.BlockSpec((1, D), const2)] + [pl.BlockSpec(w.shape, const2) for w in (wa, wb, wo, wq)],
        out_specs=[row(D), row(D), row(nq)],
        out_shape=[jax.ShapeDtypeStruct((T, D), jnp.float32), jax.ShapeDtypeStruct((T, D), jnp.float32),
                   jax.ShapeDtypeStruct((T, nq), jnp.bfloat16)],
        compiler_params=_cparams(("parallel",)),
        name="merge",
    )(a, b, sg, x2d, gt, sc2, sh2, g2, wa, wb, wo, wq)


_PAIR_COUNTS = [PEER_TOPK // (a + 1) for a in range(PEER_TOPK)]
_PAIR_OFFS = [int(v) for v in np.cumsum([0] + _PAIR_COUNTS[:-1])]
_N_PAIRS = sum(_PAIR_COUNTS)
_PAIR_ROWS = -(-_N_PAIRS // SUBLANES) * SUBLANES


def _route_kernel(qp_ref, sk_ref, e_ref, g_ref, s_ref, sv_ref, si_ref, cand_ref, pay_ref, gv_ref):
    TT = qp_ref.shape[0]
    half = PEER_KEY_DIM // 2
    NC = PEER_HEADS * 2
    NEG_INF = -jnp.inf

    for c in range(NC):
        s_ref[c] = _dot_nt(sk_ref[c // 2, c % 2], qp_ref[:, c * half:(c + 1) * half])

    rowk = lax.broadcasted_iota(jnp.int32, (PEER_N_KEYS, TT), 0)

    def key_round(r, carry):
        for c in range(NC):
            s = s_ref[c]
            m = jnp.max(s, axis=0, keepdims=True)
            am = jnp.min(jnp.where(s == m, rowk, PEER_N_KEYS), axis=0, keepdims=True)
            sv_ref[c, pl.ds(r, 1), :] = m
            si_ref[c, pl.ds(r, 1), :] = am
            s_ref[c] = jnp.where(rowk == am, NEG_INF, s)
        return carry

    lax.fori_loop(0, PEER_TOPK, key_round, 0)

    for h in range(PEER_HEADS):
        sv0, sv1 = sv_ref[2 * h], sv_ref[2 * h + 1]
        si0, si1 = si_ref[2 * h], si_ref[2 * h + 1]
        cand_ref[h, _PAIR_ROWS - SUBLANES:_PAIR_ROWS, :] = jnp.full((SUBLANES, TT), NEG_INF, jnp.float32)
        pay_ref[h, _PAIR_ROWS - SUBLANES:_PAIR_ROWS, :] = jnp.zeros((SUBLANES, TT), jnp.int32)
        for a in range(PEER_TOPK):
            n, off = _PAIR_COUNTS[a], _PAIR_OFFS[a]
            cand_ref[h, off:off + n, :] = sv0[a:a + 1, :] + sv1[0:n, :]
            pay_ref[h, off:off + n, :] = si0[a:a + 1, :] * PEER_N_KEYS + si1[0:n, :]

    rowp = lax.broadcasted_iota(jnp.int32, (_PAIR_ROWS, TT), 0)

    def pair_round(r, carry):
        for h in range(PEER_HEADS):
            s = cand_ref[h]
            m = jnp.max(s, axis=0, keepdims=True)
            am = jnp.min(jnp.where(s == m, rowp, _PAIR_ROWS), axis=0, keepdims=True)
            hit = rowp == am
            gv_ref[h, pl.ds(r, 1), :] = m
            e_ref[pl.ds(h * PEER_TOPK + r, 1), :] = jnp.sum(jnp.where(hit, pay_ref[h], 0), axis=0, keepdims=True)
            cand_ref[h] = jnp.where(hit, NEG_INF, s)
        return carry

    lax.fori_loop(0, PEER_TOPK, pair_round, 0)

    for h in range(PEER_HEADS):
        gv = gv_ref[h]
        ex = jnp.exp(gv - jnp.max(gv, axis=0, keepdims=True))
        g_ref[h * PEER_TOPK:(h + 1) * PEER_TOPK, :] = ex / jnp.sum(ex, axis=0, keepdims=True)


def _route(qp, sk16, *, tt):
    T = qp.shape[0]
    nt = T // tt
    ne = PEER_HEADS * PEER_TOPK
    out = jax.ShapeDtypeStruct((nt, ne, tt), jnp.int32)
    blk = pl.BlockSpec((None, ne, tt), lambda i: (i, 0, 0))
    return pl.pallas_call(
        _route_kernel,
        grid=(nt,),
        in_specs=[pl.BlockSpec((tt, qp.shape[1]), lambda i: (i, 0)),
                  pl.BlockSpec(sk16.shape, lambda i: (0, 0, 0, 0))],
        out_specs=[blk, blk],
        out_shape=[out, jax.ShapeDtypeStruct(out.shape, jnp.float32)],
        scratch_shapes=[pltpu.VMEM((PEER_HEADS * 2, PEER_N_KEYS, tt), jnp.float32),
                        pltpu.VMEM((PEER_HEADS * 2, PEER_TOPK, tt), jnp.float32),
                        pltpu.VMEM((PEER_HEADS * 2, PEER_TOPK, tt), jnp.int32),
                        pltpu.VMEM((PEER_HEADS, _PAIR_ROWS, tt), jnp.float32),
                        pltpu.VMEM((PEER_HEADS, _PAIR_ROWS, tt), jnp.int32),
                        pltpu.VMEM((PEER_HEADS, PEER_TOPK, tt), jnp.float32)],
        compiler_params=_cparams(("parallel",)),
        name="peer_route",
    )(qp, sk16)


def _peer_kernel(ids_ref, g_ref, h2_ref, xm_ref, gt_ref, w_hbm, o_ref, *scratch):
    TP, D = h2_ref.shape
    NE = g_ref.shape[1]
    HS = D // 2 // LANES
    NG = TP // PEER_GROUP
    bf = jnp.bfloat16
    wbufs, sem = scratch[:-1], scratch[-1]

    def issue(grp, half):
        for p in range(PEER_GROUP):
            base = (grp * PEER_GROUP + p) * NE
            slot = half * PEER_GROUP + p
            for k in range(NE):
                src = pl.multiple_of(ids_ref[base + k], PEER_ROWS)
                pltpu.make_async_copy(w_hbm.at[pl.ds(src, PEER_ROWS)],
                                      wbufs[slot].at[pl.ds(k * PEER_ROWS, PEER_ROWS)],
                                      sem.at[slot]).start(priority=k % 2)

    def wait(slot):
        pltpu.make_async_copy(w_hbm.at[pl.ds(0, NE * PEER_ROWS)], wbufs[slot], sem.at[slot]).wait()

    def table(slot, r0):
        lo, hi = [], []
        for s in range(HS):
            w = wbufs[slot][pl.ds(r0 + s, NE, stride=PEER_ROWS), :]
            lo.append(pltpu.bitcast(w << 16, jnp.float32).astype(bf))
            hi.append(pltpu.bitcast(w & jnp.int32(-65536), jnp.float32).astype(bf))
        return jnp.concatenate(lo + hi, axis=1)

    for g0 in range(PEER_SETS - 1):
        issue(g0, g0)

    def ring(i, carry):
        for r in range(PEER_SETS):
            group(PEER_SETS * i + r, r, True)
        return carry

    def group(grp, half, issue_ahead):
        slots = [half * PEER_GROUP + p for p in range(PEER_GROUP)]
        toks = [grp * PEER_GROUP + p for p in range(PEER_GROUP)]
        for slot in slots:
            wait(slot)
        if issue_ahead:
            issue(grp + PEER_SETS - 1, (half + PEER_SETS - 1) % PEER_SETS)
        acts = []
        for t, slot in zip(toks, slots):
            x = jnp.broadcast_to(h2_ref[pl.ds(t, 1), :], (SUBLANES, D)).astype(bf)
            acts.append(_dot_nt(x, table(slot, 0))[0:1, :])
        ys = []
        for t, slot, act in zip(toks, slots, acts):
            c = g_ref[pl.ds(t, 1), :] * jax.nn.gelu(act)
            c8 = jnp.broadcast_to(c, (SUBLANES, NE)).astype(bf)
            ys.append(_bdot(c8, table(slot, HS))[0:1, :])
        for t, y in zip(toks, ys):
            gt = gt_ref[pl.ds(t, 1), :] if gt_ref.shape[0] == TP else gt_ref[...]
            o_ref[pl.ds(t, 1), :] = xm_ref[pl.ds(t, 1), :] + gt * y

    n_ring = (NG - (PEER_SETS - 1)) // PEER_SETS
    lax.fori_loop(0, n_ring, ring, 0)
    for grp in range(n_ring * PEER_SETS, NG):
        group(grp, grp % PEER_SETS, grp + PEER_SETS - 1 < NG)


def _peer(ids, g, h2, xm, gt, wtab, *, tp):
    T, D = h2.shape
    NE = ids.shape[1]
    nt = T // tp
    n_mod, R, _ = gt.shape
    tiles_per_mod = nt // n_mod
    row = lambda w: pl.BlockSpec((tp, w), lambda i: (i, 0))
    return pl.pallas_call(
        _peer_kernel,
        grid=(nt,),
        in_specs=[pl.BlockSpec((tp * NE,), lambda i: (i,), memory_space=pltpu.SMEM),
                  row(NE), row(D), row(D),
                  pl.BlockSpec((None, R, D), lambda i: (i // tiles_per_mod, 0, 0)),
                  pl.BlockSpec(memory_space=pl.ANY)],
        out_specs=row(D),
        out_shape=jax.ShapeDtypeStruct((T, D), jnp.float32),
        scratch_shapes=[pltpu.VMEM((NE * PEER_ROWS, LANES), jnp.int32)] * (PEER_SETS * PEER_GROUP)
        + [pltpu.SemaphoreType.DMA((PEER_SETS * PEER_GROUP,))],
        compiler_params=_cparams(("arbitrary",)),
        name="peer_gather",
    )(ids.reshape(-1), g, h2, xm, gt, wtab)


def _pack_bf16_pairs(t):
    E, D = t.shape
    b = lax.bitcast_convert_type(t.astype(jnp.bfloat16), jnp.uint16).astype(jnp.uint32)
    w = b[:, :D // 2] | (b[:, D // 2:] << 16)
    return lax.bitcast_convert_type(w, jnp.int32).reshape(E, D // 2 // LANES, LANES)


def _rope_tables(pos, rot_dim, width):
    half = rot_dim // 2
    inv_freq = ROPE_THETA ** (-jnp.arange(half, dtype=jnp.float32) / half)
    ang = pos.astype(jnp.float32)[..., None] * inv_freq
    cos, sin = jnp.cos(ang), jnp.sin(ang)
    w = np.arange(LANES) % width
    first, second = w < half, (w >= half) & (w < rot_dim)
    src = np.where(first, w, np.where(second, w - half, 0))
    c = jnp.where(first | second, cos[:, src], 1.0)
    sa = jnp.where(first, -sin[:, src], 0.0)
    sb = jnp.where(second, sin[:, src], 0.0)
    return c, sa, sb


def _gmlp_mask():
    i = np.arange(GMLP_CHUNK)
    return (i[None, :] // CHUNK) <= (i[:, None] // CHUNK)


def _rows_per_seq(v, reps, tm):
    nb, d = v.shape
    return jnp.repeat(v, reps, axis=0).reshape(nb * reps // tm, tm, d)


def _layer(xp, xs, cache_k, cache_v, cache_kidx, c_prompt, c_sample, ada_w, ada_b, norm1_g, norm2_g, w_in,
           q_norm_g, k_norm_g, kidx_norm_g, gmlp_v_norm_g, gmlp_ws, gmlp_b, w_branch_a, w_branch_b, w_out,
           peer_wq, peer_subkeys, peer_u, peer_v):
    B, S, D = xp.shape
    DB, DS, _ = xs.shape
    P = cache_k.shape[1]
    Tp, Ts = B * S, DB * DS
    tm_p, tm_s = 2 * GMLP_CHUNK, GMLP_CHUNK
    assert S % tm_p == 0 and GMLP_CHUNK % DS == 0 and Ts % tm_s == 0 and DS <= CHUNK and P % CHUNK == 0
    tp_p, tp_s = min(PEER_TOK, S), min(PEER_TOK, Ts)
    assert S % tp_p == 0 and Ts % tp_s == 0
    assert tp_p % PEER_GROUP == 0 and tp_s % PEER_GROUP == 0 and min(tp_p, tp_s) // PEER_GROUP >= PEER_SETS
    bf = jnp.bfloat16
    f32 = jnp.float32

    nrow = B + DB
    npad = -nrow % SUBLANES
    c_all = jnp.concatenate([c_prompt, c_sample, jnp.zeros((npad, D), f32)], axis=0)
    mod = _adaln(c_all, ada_w, ada_b)
    mods = jnp.split(mod, 6, axis=-1)
    mp = [m[:B].reshape(B, 1, D) for m in mods]
    ms = [_rows_per_seq(m[B:B + DB], DS, tm_s) for m in mods]

    sizes = (GMLP_WIDTH, GMLP_WIDTH, N_HEADS * HEAD_DIM, N_KV_HEADS * HEAD_DIM, N_KV_HEADS * HEAD_DIM,
             N_IDX_HEADS * IDX_DIM, IDX_DIM, N_IDX_HEADS, 2 * D)
    pts = [int(s) for s in np.cumsum(sizes)[:-1]]
    wu, wv, wq, wk, wva, wqi, wki, wwi, wg = jnp.split(w_in.astype(bf), pts, axis=-1)
    wkw = jnp.concatenate([wki, wwi, jnp.zeros((D, LANES - IDX_DIM - N_IDX_HEADS), bf)], axis=-1)
    wts = (wu, wv, wq, wk, wva, wqi, wg, wkw)
    kig = jnp.concatenate([kidx_norm_g, jnp.zeros((LANES - IDX_DIM,), f32)]).reshape(1, LANES)
    gains = (q_norm_g.reshape(1, HEAD_DIM), k_norm_g.reshape(1, HEAD_DIM), kig, gmlp_v_norm_g.reshape(1, GMLP_WIDTH))
    g1 = norm1_g.reshape(1, D)
    g2 = norm2_g.reshape(1, D)

    wmask = jnp.where(_gmlp_mask()[None], gmlp_ws, 0.0)
    wm_p = wmask.astype(bf)
    gb_p = jnp.repeat(gmlp_b.T, GMLP_GROUP_DIM, axis=1)
    reps = GMLP_CHUNK // DS
    eye = jnp.eye(reps, dtype=f32)
    wm_s = jnp.einsum("ab,gij->gaibj", eye, wmask[:, :DS, :DS]).reshape(GMLP_GROUPS, GMLP_CHUNK, GMLP_CHUNK).astype(bf)
    gb_s = jnp.tile(gb_p[:DS], (reps, 1))

    pos_p = jnp.arange(S)
    pos_s = P + (jnp.arange(tm_s) % DS)
    tabs_p = _rope_tables(pos_p, ROT_DIM, HEAD_DIM) + _rope_tables(pos_p, IDX_ROT_DIM, IDX_DIM)
    tabs_s = _rope_tables(pos_s, ROT_DIM, HEAD_DIM) + _rope_tables(pos_s, IDX_ROT_DIM, IDX_DIM)

    x2p = xp.reshape(Tp, D)
    x2s = xs.reshape(Ts, D)

    (a_p, q_p, k32_p, k16_p, va32_p, va16_p, qi_p, kw32_p, ki16_p, sg_p) = _inproj(
        x2p, mp[1], mp[0], g1, wts, tabs_p, gains, wm_p, gb_p, tm=tm_p, tab_blocks=S // tm_p, emit_v=False)
    wit_p = kw32_p[:, IDX_DIM:IDX_DIM + N_IDX_HEADS].reshape(B, S, N_IDX_HEADS).transpose(0, 2, 1)
    vt_p = va16_p.reshape(B, S, -1).transpose(0, 2, 1)
    b_p = _dsa(qi_p, wit_p, q_p, ki16_p.reshape(B, S, LANES), k16_p.reshape(B, S, -1), vt_p,
               ksel=min(TOPK_MAX, S // 4), kt=min(KT_PROMPT, S), vis_step=QBLK, vis_base=QBLK,
               lim_lo=CHUNK, lim_hi=2 * CHUNK)

    (a_s, v_s, q_s, k32_s, k16_s, va32_s, va16_s, qi_s, kw32_s, ki16_s, sg_s) = _inproj(
        x2s, ms[1], ms[0], g1, wts, tabs_s, gains, wm_s, gb_s, tm=tm_s, tab_blocks=1, emit_v=True)
    Lk = P + DS
    Lpad = -(-Lk // KT_SAMPLE) * KT_SAMPLE

    def pad_q(a):
        w = a.shape[-1]
        return jnp.pad(a.reshape(DB, DS, w), ((0, 0), (0, QBLK - DS), (0, 0))).reshape(DB * QBLK, w)

    def cat_keys(cache, new, w):
        return jnp.concatenate([cache.reshape(DB, P, -1).astype(bf), new.reshape(DB, DS, -1),
                                jnp.zeros((DB, Lpad - Lk, w), bf)], axis=1)

    wit_s = jnp.pad(kw32_s[:, IDX_DIM:IDX_DIM + N_IDX_HEADS].reshape(DB, DS, N_IDX_HEADS).transpose(0, 2, 1),
                    ((0, 0), (0, 0), (0, QBLK - DS)))
    kidx_cache = jnp.pad(cache_kidx, ((0, 0), (0, 0), (0, LANES - IDX_DIM)))
    ki_all = cat_keys(kidx_cache, ki16_s, LANES)
    k_all = cat_keys(cache_k, k16_s, N_KV_HEADS * HEAD_DIM)
    vt_all = cat_keys(cache_v, va16_s, N_KV_HEADS * HEAD_DIM).transpose(0, 2, 1)
    b_s = _dsa(pad_q(qi_s), wit_s, pad_q(q_s), ki_all, k_all, vt_all,
               ksel=min(TOPK_MAX, Lk // 4), kt=KT_SAMPLE, vis_step=0, vis_base=Lpad, lim_lo=Lk, lim_hi=Lk,
               q_valid=DS)
    b_s = b_s.reshape(DB, QBLK, -1)[:, :DS].reshape(Ts, -1)

    wa, wb, wo, wpq = w_branch_a.astype(bf), w_branch_b.astype(bf), w_out.astype(bf), peer_wq.astype(bf)
    sk16 = peer_subkeys.astype(bf)
    wtab = jnp.concatenate([_pack_bf16_pairs(peer_u), _pack_bf16_pairs(peer_v)], axis=1).reshape(-1, LANES)

    def tail(a, b, sg, x2d, m, tm):
        xm, h2, qp = _merge(a, b, sg, x2d, m[2], m[4], m[3], g2, wa, wb, wo, wpq, tm=tm)
        e, g = _route(qp, sk16, tt=LANES)
        T = x2d.shape[0]
        ids = e.transpose(0, 2, 1).reshape(T, PEER_HEADS * PEER_TOPK) * PEER_ROWS
        gw = g.transpose(0, 2, 1).reshape(T, PEER_HEADS * PEER_TOPK)
        return ids, gw, h2, xm

    ids_p, gw_p, h2_p, xm_p = tail(a_p, b_p, sg_p, x2p, mp, tm_p)
    out_p = _peer(ids_p, gw_p, h2_p, xm_p, mp[5], wtab, tp=tp_p)
    ms_peer = [_rows_per_seq(m[B:B + DB], DS, tp_s) for m in mods]
    ids_s, gw_s, h2_s, xm_s = tail(a_s, b_s, sg_s, x2s, ms, tm_s)
    out_s = _peer(ids_s, gw_s, h2_s, xm_s, ms_peer[5], wtab, tp=tp_s)

    new = (k32_p.reshape(B, S, N_KV_HEADS, HEAD_DIM), va32_p.reshape(B, S, N_KV_HEADS, HEAD_DIM),
           kw32_p[:, :IDX_DIM].reshape(B, S, IDX_DIM),
           k32_s.reshape(DB, DS, N_KV_HEADS, HEAD_DIM), va32_s.reshape(DB, DS, N_KV_HEADS, HEAD_DIM),
           kw32_s[:, :IDX_DIM].reshape(DB, DS, IDX_DIM), v_s.reshape(DB, DS, GMLP_WIDTH))
    return out_p.reshape(B, S, D), out_s.reshape(DB, DS, D), new


def kernel(x_prompt, x_sample, cache_k, cache_v, cache_kidx, c_prompt, c_sample, ada_w, ada_b, norm1_g, norm2_g,
           w_in, q_norm_g, k_norm_g, kidx_norm_g, gmlp_v_norm_g, gmlp_ws, gmlp_b, w_branch_a, w_branch_b, w_out,
           peer_wq, peer_subkeys, peer_u, peer_v):
    xp, xs = x_prompt, x_sample
    per_layer = []
    for l in range(ada_w.shape[0]):
        xp, xs, new = _layer(xp, xs, cache_k[l], cache_v[l], cache_kidx[l], c_prompt, c_sample, ada_w[l], ada_b[l],
                             norm1_g[l], norm2_g[l], w_in[l], q_norm_g[l], k_norm_g[l], kidx_norm_g[l],
                             gmlp_v_norm_g[l], gmlp_ws[l], gmlp_b[l], w_branch_a[l], w_branch_b[l], w_out[l],
                             peer_wq[l], peer_subkeys[l], peer_u[l], peer_v[l])
        per_layer.append(new)
    stacked = tuple(jnp.stack([n[i] for n in per_layer]) for i in range(7))
    return (xp, xs) + stacked
```
